```python
import jax, jax.numpy as jnp
from jax import lax
import numpy as np

D_MODEL = 1024
BATCH = 16
SEQ = 4096
DEPTH = 1

MIX_WIDTH = D_MODEL
ATTN_HEAD_DIM = 64
ATTN_WIDTH = MIX_WIDTH // 2
ATTN_HEADS = ATTN_WIDTH // ATTN_HEAD_DIM
ATTN_KV_HEADS = ATTN_HEADS // 4
ATTN_KV_WIDTH = ATTN_KV_HEADS * ATTN_HEAD_DIM
WINDOW = 128
ATTN_BLOCK = 128
HGRN_KEY_DIM = 128
HGRN_WIDTH = MIX_WIDTH - ATTN_WIDTH
HGRN_HEADS = HGRN_WIDTH // HGRN_KEY_DIM
HGRN_VAL_DIM = HGRN_WIDTH // HGRN_HEADS
HGRN_KEY_WIDTH = HGRN_HEADS * HGRN_KEY_DIM
CHUNK = 64
IN_SPLITS = (ATTN_WIDTH, ATTN_KV_WIDTH, ATTN_KV_WIDTH, HGRN_KEY_WIDTH, HGRN_KEY_WIDTH, HGRN_WIDTH, HGRN_WIDTH)
IN_COLS = sum(IN_SPLITS)
IN_OFFSETS = tuple(int(s) for s in np.cumsum(IN_SPLITS)[:-1])
D_FF = ((8 * D_MODEL) // 3 + 255) // 256 * 256
CONV_WIDTH = 3
N_MOD = 6
EPS = 1e-6

kernel_name = "hymba_style_swa_sink_hgrn2_convffn_adaln"


def rms_norm(x, gain):
    xf = x.astype(jnp.float32)
    y = xf * lax.rsqrt(jnp.mean(xf * xf, axis=-1, keepdims=True) + EPS)
    return (y * gain.astype(jnp.float32)).astype(x.dtype)


def sliding_window_gqa_with_sinks(q, k, v, sinks):
    B, T = q.shape[0], q.shape[1]
    nb = T // ATTN_BLOCK
    G = ATTN_HEADS // ATTN_KV_HEADS
    qb = q.reshape(B, nb, ATTN_BLOCK, ATTN_KV_HEADS, G, ATTN_HEAD_DIM)

    def band_keys(t):
        prev = jnp.pad(t, ((0, 0), (ATTN_BLOCK, 0), (0, 0), (0, 0)))[:, :T]
        shp = (B, nb, ATTN_BLOCK, ATTN_KV_HEADS, ATTN_HEAD_DIM)
        return jnp.concatenate([prev.reshape(shp), t.reshape(shp)], axis=2)

    kb, vb = band_keys(k), band_keys(v)
    scores = jnp.einsum('bnqhgd,bnkhd->bnhgqk', qb.astype(jnp.float32), kb.astype(jnp.float32)) * (ATTN_HEAD_DIM ** -0.5)
    qi = jnp.arange(ATTN_BLOCK)[:, None]
    ki = jnp.arange(2 * ATTN_BLOCK)[None, :]
    rel = ATTN_BLOCK + qi - ki
    band = (rel >= 0) & (rel < WINDOW)
    not_pad = (jnp.arange(nb)[:, None, None] > 0) | (ki >= ATTN_BLOCK)[None]
    mask = band[None] & not_pad
    scores = jnp.where(mask[None, :, None, None], scores, -jnp.inf)
    sink = sinks.astype(jnp.float32).reshape(1, 1, ATTN_KV_HEADS, G, 1, 1)
    m = jnp.maximum(scores.max(axis=-1, keepdims=True), sink)
    e = jnp.exp(scores - m)
    p = e / (e.sum(axis=-1, keepdims=True) + jnp.exp(sink - m))
    out = jnp.einsum('bnhgqk,bnkhd->bnqhgd', p, vb.astype(jnp.float32))
    return out.reshape(B, T, ATTN_WIDTH).astype(q.dtype)


def hgrn2_chunkwise(q, f_logit, i, lower_bound):
    B, T, H, Dk = q.shape
    Dv = i.shape[-1]
    nc = T // CHUNK
    qf = jax.nn.silu(q.astype(jnp.float32))
    f = lower_bound + (1 - lower_bound) * jax.nn.sigmoid(f_logit.astype(jnp.float32))
    kf = 1 - f
    g = jnp.log(f)
    vf = i.astype(jnp.float32)

    def to_chunks(t):
        return t.reshape(B, nc, CHUNK, H, t.shape[-1]).transpose(1, 0, 3, 2, 4)

    qc, kc, vc = to_chunks(qf), to_chunks(kf), to_chunks(vf)
    bc = jnp.cumsum(to_chunks(g), axis=3)
    causal = jnp.tril(jnp.ones((CHUNK, CHUNK), dtype=bool))

    def step(S, inp):
        q_, k_, v_, b_ = inp
        b_last = b_[:, :, -1:, :]
        o_inter = jnp.einsum('bhtk,bhkv->bhtv', q_ * jnp.exp(b_), S)
        diff = b_[:, :, :, None, :] - b_[:, :, None, :, :]
        decay = jnp.exp(jnp.where(causal[:, :, None], diff, -jnp.inf))
        scores = jnp.einsum('bhtk,bhtsk,bhsk->bhts', q_, decay, k_)
        o_intra = jnp.einsum('bhts,bhsv->bhtv', scores, v_)
        S_new = jnp.exp(b_last[:, :, 0, :])[..., None] * S + jnp.einsum('bhsk,bhsv->bhkv', k_ * jnp.exp(b_last - b_), v_)
        return S_new, o_inter + o_intra

    S0 = jnp.zeros((B, H, Dk, Dv), jnp.float32)
    _, o = lax.scan(step, S0, (qc, kc, vc, bc))
    return o.transpose(1, 0, 3, 2, 4).reshape(B, T, H, Dv).astype(q.dtype)


def causal_depthwise_conv(u, w, b):
    out = lax.conv_general_dilated(
        u, w[:, None, :].astype(u.dtype), window_strides=(1,),
        padding=[(CONV_WIDTH - 1, 0)], dimension_numbers=('NWC', 'WIO', 'NWC'),
        feature_group_count=u.shape[-1])
    return out + b


def setup_inputs(seed: int = 0) -> dict:
    key = jax.random.key(seed)
    ks = jax.random.split(key, 18)
    f32 = jnp.float32

    def normal(k, shape, scale):
        return jax.random.normal(k, shape, f32) * scale

    def gain(k, shape):
        return 1.0 + 0.05 * jax.random.normal(k, shape, f32)

    return {
        "x": normal(ks[0], (BATCH, SEQ, D_MODEL), 1.0),
        "c": normal(ks[1], (BATCH, D_MODEL), 1.0),
        "ada_w": normal(ks[2], (DEPTH, D_MODEL, N_MOD * D_MODEL), 0.5 * D_MODEL ** -0.5),
        "ada_b": normal(ks[3], (DEPTH, N_MOD * D_MODEL), 0.02),
        "mix_norm_g": gain(ks[4], (DEPTH, D_MODEL)),
        "w_in": normal(ks[5], (DEPTH, D_MODEL, IN_COLS), D_MODEL ** -0.5),
        "b_in": normal(ks[6], (DEPTH, IN_COLS), 0.02),
        "attn_sinks": normal(ks[7], (DEPTH, ATTN_HEADS), 0.5),
        "attn_out_g": gain(ks[8], (DEPTH, ATTN_WIDTH)),
        "hgrn_lb_logits": normal(ks[9], (DEPTH + 1, HGRN_KEY_WIDTH), 0.5),
        "hgrn_out_g": gain(ks[10], (DEPTH, HGRN_WIDTH)),
        "w_out": normal(ks[11], (DEPTH, MIX_WIDTH, D_MODEL), MIX_WIDTH ** -0.5),
        "ffn_norm_g": gain(ks[12], (DEPTH, D_MODEL)),
        "w_up": normal(ks[13], (DEPTH, D_MODEL, 2 * D_FF), D_MODEL ** -0.5),
        "conv_w": normal(ks[14], (DEPTH, CONV_WIDTH, D_FF), CONV_WIDTH ** -0.5),
        "conv_b": normal(ks[15], (DEPTH, D_FF), 0.02),
        "w_down": normal(ks[16], (DEPTH, D_FF, D_MODEL), D_FF ** -0.5),
        "final_norm_g": gain(ks[17], (D_MODEL,)),
    }


def reference(x, c, ada_w, ada_b, mix_norm_g, w_in, b_in, attn_sinks, attn_out_g, hgrn_lb_logits,
              hgrn_out_g, w_out, ffn_norm_g, w_up, conv_w, conv_b, w_down, final_norm_g):
    B, T = x.shape[0], x.shape[1]
    lower_bounds = jnp.cumsum(jax.nn.softmax(hgrn_lb_logits.astype(jnp.float32), axis=0), axis=0)
    cond = jax.nn.silu(c)
    for l in range(DEPTH):
        mod = cond @ ada_w[l] + ada_b[l]
        shift_m, scale_m, gate_m, shift_f, scale_f, gate_f = jnp.split(mod[:, None, :], N_MOD, axis=-1)

        h = rms_norm(x, mix_norm_g[l]) * (1 + scale_m) + shift_m
        proj = h @ w_in[l] + b_in[l]
        q_a, k_a, v_a, q_h, f_h, i_h, g_h = jnp.split(proj, IN_OFFSETS, axis=-1)
        attn = sliding_window_gqa_with_sinks(
            q_a.reshape(B, T, ATTN_HEADS, ATTN_HEAD_DIM),
            k_a.reshape(B, T, ATTN_KV_HEADS, ATTN_HEAD_DIM),
            v_a.reshape(B, T, ATTN_KV_HEADS, ATTN_HEAD_DIM),
            attn_sinks[l])
        attn = rms_norm(attn, attn_out_g[l])
        rec = hgrn2_chunkwise(
            q_h.reshape(B, T, HGRN_HEADS, HGRN_KEY_DIM),
            f_h.reshape(B, T, HGRN_HEADS, HGRN_KEY_DIM),
            i_h.reshape(B, T, HGRN_HEADS, HGRN_VAL_DIM),
            lower_bounds[l].reshape(HGRN_HEADS, HGRN_KEY_DIM))
        rec = rms_norm(rec, hgrn_out_g[l].reshape(HGRN_HEADS, HGRN_VAL_DIM)) * jax.nn.silu(g_h.reshape(B, T, HGRN_HEADS, HGRN_VAL_DIM))
        mixed = jnp.concatenate([attn, rec.reshape(B, T, HGRN_WIDTH)], axis=-1) @ w_out[l]
        x = x + gate_m * mixed

        h = rms_norm(x, ffn_norm_g[l]) * (1 + scale_f) + shift_f
        u, v = jnp.split(h @ w_up[l], 2, axis=-1)
        u = causal_depthwise_conv(u, conv_w[l], conv_b[l])
        x = x + gate_f * ((jax.nn.silu(u) * v) @ w_down[l])
    return rms_norm(x, final_norm_g)
```

```python
import functools

import numpy as np
import jax
import jax.numpy as jnp
from jax import lax
from jax.experimental import pallas as pl
from jax.experimental.pallas import tpu as pltpu

D_MODEL = 1024
ATTN_HEAD_DIM = 64
ATTN_WIDTH = 512
ATTN_HEADS = 8
ATTN_KV_HEADS = 2
ATTN_KV_WIDTH = 128
WINDOW = 128
HGRN_HEADS = 4
HGRN_DIM = 128
HGRN_WIDTH = 512
IN_SPLITS = (ATTN_WIDTH, ATTN_KV_WIDTH, ATTN_KV_WIDTH, HGRN_WIDTH, HGRN_WIDTH, HGRN_WIDTH, HGRN_WIDTH)
IN_OFFSETS = tuple(int(s) for s in np.cumsum((0,) + IN_SPLITS))
IN_COLS = IN_OFFSETS[-1]
D_FF = 2816
N_MOD = 6
EPS = 1e-6

BLOCK = 128
N_LEVELS = 7
LANES_V7X = 128
VMEM_LIMIT_BYTES = 56 * 1024 * 1024

F32 = jnp.float32
BF16 = jnp.bfloat16


def _dot(a, b):
    return jnp.dot(a, b, preferred_element_type=F32)


def _dot_nt(a, b):
    return lax.dot_general(a, b, (((1,), (1,)), ((), ())), preferred_element_type=F32)


def _split_bf16(v):
    hi = v.astype(BF16)
    lo = (v - hi.astype(F32)).astype(BF16)
    return hi, lo


def _rms_rows(v):
    return v * lax.rsqrt(jnp.mean(v * v, axis=-1, keepdims=True) + EPS)


def _silu(v):
    return v * jax.nn.sigmoid(v)


def _decay_range_matrix():
    t = np.arange(BLOCK)[:, None]
    r = np.arange(BLOCK)[None, :]
    mats = [(r <= t), (r > t)]
    for l in range(N_LEVELS):
        half = BLOCK >> (l + 1)
        ref = (t // (2 * half)) * (2 * half) + half - 1
        mats.append((r > np.minimum(t, ref)) & (r <= np.maximum(t, ref)))
    return np.concatenate(mats, axis=0).astype(np.float32)


def _ada_kernel(c_ref, w_ref, b_ref, o_ref):
    cond = _silu(c_ref[...])
    c_hi, c_lo = _split_bf16(cond)
    w_hi, w_lo = _split_bf16(w_ref[...])
    o_ref[...] = _dot(c_hi, w_hi) + _dot(c_lo, w_hi) + _dot(c_hi, w_lo) + b_ref[...]


def _ada_call(c, w, b):
    bsz = c.shape[0]
    n = w.shape[1]
    tn = 512
    return pl.pallas_call(
        _ada_kernel,
        grid=(n // tn,),
        in_specs=[
            pl.BlockSpec((bsz, D_MODEL), lambda j: (0, 0)),
            pl.BlockSpec((D_MODEL, tn), lambda j: (0, j)),
            pl.BlockSpec((1, tn), lambda j: (0, j)),
        ],
        out_specs=pl.BlockSpec((bsz, tn), lambda j: (0, j)),
        out_shape=jax.ShapeDtypeStruct((bsz, n), F32),
        name="ada_mod",
    )(c, w, b)


def _mixer_kernel(sink_ref, x_ref, mod_ref, g_mix_ref, w_in_ref, b_in_ref, attn_g_ref, lb_logit_ref,
                  hgrn_g_ref, w_out_ref, dmat_ref, o_ref, kv_ref, state_ref):
    j = pl.program_id(1)

    @pl.when(j == 0)
    def _():
        kv_ref[...] = jnp.zeros_like(kv_ref)
        state_ref[...] = jnp.zeros_like(state_ref)

    x = x_ref[0]
    shift = mod_ref[0, 0:1, :]
    scale = mod_ref[0, 1:2, :]
    gate = mod_ref[0, 2:3, :]
    h = (_rms_rows(x) * g_mix_ref[...] * (1.0 + scale) + shift).astype(BF16)

    def proj(i):
        lo, hi = IN_OFFSETS[i], IN_OFFSETS[i + 1]
        return _dot(h, w_in_ref[:, lo:hi]) + b_in_ref[:, lo:hi]

    q_a, k_a, v_a, q_h, f_h, i_h, g_h = (proj(i) for i in range(7))

    lane = lax.broadcasted_iota(jnp.int32, (BLOCK, LANES_V7X), 1)
    low = lane < ATTN_HEAD_DIM
    zero = jnp.zeros((BLOCK, LANES_V7X), F32)
    k_roll = pltpu.roll(k_a, ATTN_HEAD_DIM, 1)
    v_roll = pltpu.roll(v_a, ATTN_HEAD_DIM, 1)
    variants = (
        (jnp.where(low, k_a, zero), jnp.where(low, v_a, zero)),
        (jnp.where(low, zero, k_roll), jnp.where(low, zero, v_roll)),
        (jnp.where(low, k_roll, zero), jnp.where(low, v_roll, zero)),
        (jnp.where(low, zero, k_a), jnp.where(low, zero, v_a)),
    )
    for n, (kk, vv) in enumerate(variants):
        kv_ref[n, BLOCK:2 * BLOCK, :] = kk.astype(BF16)
        kv_ref[4 + n, BLOCK:2 * BLOCK, :] = vv.astype(BF16)

    qi = lax.broadcasted_iota(jnp.int32, (BLOCK, 2 * BLOCK), 0)
    ki = lax.broadcasted_iota(jnp.int32, (BLOCK, 2 * BLOCK), 1)
    valid = (ki > qi) & (ki <= qi + WINDOW) & ((j > 0) | (ki >= BLOCK))
    neg_inf = jnp.full((BLOCK, 2 * BLOCK), -jnp.inf, F32)

    pairs = []
    for p in range(ATTN_HEADS // 2):
        kvh = p // 2
        qp = (q_a[:, p * LANES_V7X:(p + 1) * LANES_V7X] * (ATTN_HEAD_DIM ** -0.5)).astype(BF16)
        acc = None
        for half in range(2):
            s = _dot_nt(qp, kv_ref[2 * kvh + half])
            s = jnp.where(valid, s, neg_inf)
            sink = sink_ref[2 * p + half]
            m = jnp.maximum(jnp.max(s, axis=-1, keepdims=True), sink)
            e = jnp.exp(s - m)
            denom = jnp.sum(e, axis=-1, keepdims=True) + jnp.exp(sink - m)
            o = _dot(e.astype(BF16), kv_ref[4 + 2 * kvh + half]) * (1.0 / denom)
            acc = o if acc is None else acc + o
        pairs.append(acc)
    attn = jnp.concatenate(pairs, axis=-1)
    attn = (_rms_rows(attn) * attn_g_ref[...]).astype(BF16)

    kv_ref[:, 0:BLOCK, :] = kv_ref[:, BLOCK:2 * BLOCK, :]

    l0 = lb_logit_ref[0:1, :]
    l1 = lb_logit_ref[1:2, :]
    lmax = jnp.maximum(l0, l1)
    e0 = jnp.exp(l0 - lmax)
    lb = e0 / (e0 + jnp.exp(l1 - lmax))
    f = lb + (1.0 - lb) * jax.nn.sigmoid(f_h)
    kf = 1.0 - f
    g_hi, g_lo = _split_bf16(jnp.log(f))
    qf = _silu(q_h)
    dmat = dmat_ref[...]
    ti = lax.broadcasted_iota(jnp.int32, (BLOCK, BLOCK), 0)
    si = lax.broadcasted_iota(jnp.int32, (BLOCK, BLOCK), 1)
    row = lax.broadcasted_iota(jnp.int32, (BLOCK, 1), 0)

    recs = []
    for hh in range(HGRN_HEADS):
        cols = slice(hh * HGRN_DIM, (hh + 1) * HGRN_DIM)
        decay = jnp.exp(_dot(dmat, g_hi[:, cols]) + _dot(dmat, g_lo[:, cols]))
        qh, kh, vh = qf[:, cols], kf[:, cols], i_h[:, cols]
        q_inter = (qh * decay[0:BLOCK]).astype(BF16)
        k_state = (kh * decay[BLOCK:2 * BLOCK]).astype(BF16)
        scores = jnp.where(ti == si, jnp.sum(qh * kh, axis=-1, keepdims=True), 0.0)
        for l in range(N_LEVELS):
            half = BLOCK >> (l + 1)
            e_l = decay[(2 + l) * BLOCK:(3 + l) * BLOCK]
            upper = (row & half) != 0
            q_l = jnp.where(upper, qh * e_l, 0.0).astype(BF16)
            k_l = jnp.where(upper, 0.0, kh * e_l).astype(BF16)
            same = (ti // (2 * half)) == (si // (2 * half))
            scores = scores + jnp.where(same, _dot_nt(q_l, k_l), 0.0)
        v_bf = vh.astype(BF16)
        state_t = state_ref[hh]
        o = _dot_nt(q_inter, state_t.astype(BF16)) + _dot(scores.astype(BF16), v_bf)
        state_ref[hh] = state_t * decay[BLOCK - 1:BLOCK] + _dot(vh.T.astype(BF16), k_state)
        o = _rms_rows(o) * hgrn_g_ref[:, cols] * _silu(g_h[:, cols])
        recs.append(o.astype(BF16))
    rec = jnp.concatenate(recs, axis=-1)

    mixed = _dot(attn, w_out_ref[0:ATTN_WIDTH, :]) + _dot(rec, w_out_ref[ATTN_WIDTH:, :])
    o_ref[0] = x + gate * mixed


def _mixer_call(x, mod, sinks, g_mix, w_in, b_in, attn_g, lb_logits, hgrn_g, w_out, dmat):
    bsz, seq, _ = x.shape
    const = lambda b, j: (0, 0)
    return pl.pallas_call(
        _mixer_kernel,
        grid=(bsz, seq // BLOCK),
        in_specs=[
            pl.BlockSpec(memory_space=pltpu.SMEM),
            pl.BlockSpec((1, BLOCK, D_MODEL), lambda b, j: (b, j, 0)),
            pl.BlockSpec((1, N_MOD, D_MODEL), lambda b, j: (b, 0, 0)),
            pl.BlockSpec((1, D_MODEL), const),
            pl.BlockSpec((D_MODEL, IN_COLS), const),
            pl.BlockSpec((1, IN_COLS), const),
            pl.BlockSpec((1, ATTN_WIDTH), const),
            pl.BlockSpec((2, HGRN_WIDTH), const),
            pl.BlockSpec((1, HGRN_WIDTH), const),
            pl.BlockSpec((D_MODEL, D_MODEL), const),
            pl.BlockSpec(((2 + N_LEVELS) * BLOCK, BLOCK), const),
        ],
        out_specs=pl.BlockSpec((1, BLOCK, D_MODEL), lambda b, j: (b, j, 0)),
        out_shape=jax.ShapeDtypeStruct(x.shape, F32),
        scratch_shapes=[
            pltpu.VMEM((8, 2 * BLOCK, LANES_V7X), BF16),
            pltpu.VMEM((HGRN_HEADS, HGRN_DIM, HGRN_DIM), F32),
        ],
        compiler_params=pltpu.CompilerParams(
            dimension_semantics=("arbitrary", "arbitrary"), vmem_limit_bytes=VMEM_LIMIT_BYTES),
        name="token_mixer",
    )(sinks, x, mod, g_mix, w_in, b_in, attn_g, lb_logits, hgrn_g, w_out, dmat)


FFN_ROWS = 512
FFN_CHUNK = 256
CONV_HALO = 8


def _ffn_kernel(x_ref, mod_ref, g_ffn_ref, w_up_ref, conv_w_ref, conv_b_ref, w_down_ref, g_fin_ref,
                o_ref, carry_ref, ubuf_ref, act_ref):
    j = pl.program_id(1)

    @pl.when(j == 0)
    def _():
        carry_ref[...] = jnp.zeros_like(carry_ref)

    x = x_ref[0]
    shift = mod_ref[0, 3:4, :]
    scale = mod_ref[0, 4:5, :]
    gate = mod_ref[0, 5:6, :]
    h = (_rms_rows(x) * g_ffn_ref[...] * (1.0 + scale) + shift).astype(BF16)

    for c in range(D_FF // FFN_CHUNK):
        cols = slice(c * FFN_CHUNK, (c + 1) * FFN_CHUNK)
        u = _dot(h, w_up_ref[:, cols])
        v = _dot(h, w_up_ref[:, D_FF + c * FFN_CHUNK:D_FF + (c + 1) * FFN_CHUNK])
        ubuf_ref[0:CONV_HALO, :] = carry_ref[:, cols]
        ubuf_ref[CONV_HALO:, :] = u
        carry_ref[:, cols] = u[FFN_ROWS - CONV_HALO:, :]
        u1 = ubuf_ref[CONV_HALO - 1:CONV_HALO - 1 + FFN_ROWS, :]
        u2 = ubuf_ref[CONV_HALO - 2:CONV_HALO - 2 + FFN_ROWS, :]
        conv = (conv_w_ref[0:1, cols] * u2 + conv_w_ref[1:2, cols] * u1 + conv_w_ref[2:3, cols] * u
                + conv_b_ref[:, cols])
        act_ref[:, cols] = (_silu(conv) * v).astype(BF16)

    y = x + gate * _dot(act_ref[...], w_down_ref[...])
    o_ref[0] = _rms_rows(y) * g_fin_ref[...]


def _ffn_call(x1, mod, g_ffn, w_up, conv_w, conv_b, w_down, g_fin):
    bsz, seq, _ = x1.shape
    const = lambda b, j: (0, 0)
    return pl.pallas_call(
        _ffn_kernel,
        grid=(bsz, seq // FFN_ROWS),
        in_specs=[
            pl.BlockSpec((1, FFN_ROWS, D_MODEL), lambda b, j: (b, j, 0)),
            pl.BlockSpec((1, N_MOD, D_MODEL), lambda b, j: (b, 0, 0)),
            pl.BlockSpec((1, D_MODEL), const),
            pl.BlockSpec((D_MODEL, 2 * D_FF), const),
            pl.BlockSpec((3, D_FF), const),
            pl.BlockSpec((1, D_FF), const),
            pl.BlockSpec((D_FF, D_MODEL), const),
            pl.BlockSpec((1, D_MODEL), const),
        ],
        out_specs=pl.BlockSpec((1, FFN_ROWS, D_MODEL), lambda b, j: (b, j, 0)),
        out_shape=jax.ShapeDtypeStruct(x1.shape, F32),
        scratch_shapes=[
            pltpu.VMEM((CONV_HALO, D_FF), F32),
            pltpu.VMEM((CONV_HALO + FFN_ROWS, FFN_CHUNK), F32),
            pltpu.VMEM((FFN_ROWS, D_FF), BF16),
        ],
        compiler_params=pltpu.CompilerParams(
            dimension_semantics=("arbitrary", "arbitrary"), vmem_limit_bytes=VMEM_LIMIT_BYTES),
        name="channel_mixer",
    )(x1, mod, g_ffn, w_up, conv_w, conv_b, w_down, g_fin)


def kernel(x, c, ada_w, ada_b, mix_norm_g, w_in, b_in, attn_sinks, attn_out_g, hgrn_lb_logits, hgrn_out_g,
           w_out, ffn_norm_g, w_up, conv_w, conv_b, w_down, final_norm_g):
    assert ada_w.shape[0] == 1 and x.shape[-1] == D_MODEL and x.shape[1] % FFN_ROWS == 0
    bsz = x.shape[0]
    mod = _ada_call(c, ada_w[0], ada_b[0][None, :]).reshape(bsz, N_MOD, D_MODEL)
    dmat = jnp.asarray(_decay_range_matrix(), dtype=BF16)
    x1 = _mixer_call(
        x, mod, attn_sinks[0], mix_norm_g[0][None, :], w_in[0].astype(BF16), b_in[0][None, :],
        attn_out_g[0][None, :], hgrn_lb_logits, hgrn_out_g[0][None, :], w_out[0].astype(BF16), dmat)
    return _ffn_call(
        x1, mod, ffn_norm_g[0][None, :], w_up[0].astype(BF16), conv_w[0], conv_b[0][None, :],
        w_down[0].astype(BF16), final_norm_g[None, :])
```

```python
import functools

import numpy as np
import jax
import jax.numpy as jnp
from jax import lax
from jax.experimental import pallas as pl
from jax.experimental.pallas import tpu as pltpu

D_MODEL = 1024
ATTN_HEAD_DIM = 64
ATTN_WIDTH = 512
ATTN_HEADS = 8
ATTN_KV_HEADS = 2
ATTN_KV_WIDTH = 128
WINDOW = 128
HGRN_HEADS = 4
HGRN_DIM = 128
HGRN_WIDTH = 512
IN_SPLITS = (ATTN_WIDTH, ATTN_KV_WIDTH, ATTN_KV_WIDTH, HGRN_WIDTH, HGRN_WIDTH, HGRN_WIDTH, HGRN_WIDTH)
IN_OFFSETS = tuple(int(s) for s in np.cumsum((0,) + IN_SPLITS))
IN_COLS = IN_OFFSETS[-1]
D_FF = 2816
N_MOD = 6
EPS = 1e-6

BLOCK = 128
N_LEVELS = 7
LANES_V7X = 128
VMEM_LIMIT_BYTES = 56 * 1024 * 1024

F32 = jnp.float32
BF16 = jnp.bfloat16


def _dot(a, b):
    return jnp.dot(a, b, preferred_element_type=F32)


def _dot_nt(a, b):
    return lax.dot_general(a, b, (((1,), (1,)), ((), ())), preferred_element_type=F32)


def _split_bf16(v):
    hi = v.astype(BF16)
    lo = (v - hi.astype(F32)).astype(BF16)
    return hi, lo


def _rms_rows(v):
    return v * lax.rsqrt(jnp.mean(v * v, axis=-1, keepdims=True) + EPS)


def _silu(v):
    return v * jax.nn.sigmoid(v)


def _cumsum_matrix():
    t = np.arange(BLOCK)[:, None]
    r = np.arange(BLOCK)[None, :]
    return (r <= t).astype(np.float32)


def _level_decay(b, g, f, half, row):
    if half == 1:
        return jnp.where((row & 1) != 0, f, 1.0)
    if half == 2:
        up1 = pltpu.roll(g, BLOCK - 1, 0)
        dn1 = pltpu.roll(g, 1, 0)
        pos = row & 3
        d = jnp.where(pos == 0, up1, jnp.where(pos == 1, 0.0, jnp.where(pos == 2, g, g + dn1)))
        return jnp.exp(d)
    blk = 2 * half
    w = b.shape[-1]
    b3 = b.reshape(BLOCK // blk, blk, w)
    beta = jnp.broadcast_to(b3[:, half - 1:half, :], b3.shape).reshape(BLOCK, w)
    return jnp.exp(-jnp.abs(b - beta))


def _pair_scores(a0, b0, a1, b1):
    z = jnp.zeros_like(b0)
    lhs = jnp.concatenate([a0, a1], axis=1)
    rhs = jnp.concatenate([jnp.concatenate([b0, z], axis=1), jnp.concatenate([z, b1], axis=1)], axis=0)
    out = _dot_nt(lhs, rhs)
    return out[:, :BLOCK], out[:, BLOCK:]


def _ada_kernel(c_ref, w_ref, b_ref, o_ref):
    cond = _silu(c_ref[...])
    c_hi, c_lo = _split_bf16(cond)
    w_hi, w_lo = _split_bf16(w_ref[...])
    o_ref[...] = _dot(c_hi, w_hi) + _dot(c_lo, w_hi) + _dot(c_hi, w_lo) + b_ref[...]


def _ada_call(c, w, b):
    bsz = c.shape[0]
    n = w.shape[1]
    tn = 512
    return pl.pallas_call(
        _ada_kernel,
        grid=(n // tn,),
        in_specs=[
            pl.BlockSpec((bsz, D_MODEL), lambda j: (0, 0)),
            pl.BlockSpec((D_MODEL, tn), lambda j: (0, j)),
            pl.BlockSpec((1, tn), lambda j: (0, j)),
        ],
        out_specs=pl.BlockSpec((bsz, tn), lambda j: (0, j)),
        out_shape=jax.ShapeDtypeStruct((bsz, n), F32),
        name="ada_mod",
    )(c, w, b)


def _mixer_kernel(sink_ref, x_ref, mod_ref, g_mix_ref, w_in_ref, b_in_ref, attn_g_ref, lb_logit_ref,
                  hgrn_g_ref, w_out_ref, tri_ref, o_ref, kv_ref, state_ref):
    j = pl.program_id(1)

    @pl.when(j == 0)
    def _():
        kv_ref[...] = jnp.zeros_like(kv_ref)
        state_ref[...] = jnp.zeros_like(state_ref)

    x = x_ref[0]
    shift = mod_ref[0, 0:1, :]
    scale = mod_ref[0, 1:2, :]
    gate = mod_ref[0, 2:3, :]
    h = (_rms_rows(x) * g_mix_ref[...] * (1.0 + scale) + shift).astype(BF16)

    def proj(i):
        lo, hi = IN_OFFSETS[i], IN_OFFSETS[i + 1]
        return _dot(h, w_in_ref[:, lo:hi]) + b_in_ref[:, lo:hi]

    q_a, k_a, v_a, q_h, f_h, i_h, g_h = (proj(i) for i in range(7))

    lane = lax.broadcasted_iota(jnp.int32, (BLOCK, LANES_V7X), 1)
    low = lane < ATTN_HEAD_DIM
    zero = jnp.zeros((BLOCK, LANES_V7X), F32)
    k_roll = pltpu.roll(k_a, ATTN_HEAD_DIM, 1)
    v_roll = pltpu.roll(v_a, ATTN_HEAD_DIM, 1)
    variants = (
        (jnp.where(low, k_a, zero), jnp.where(low, v_a, zero)),
        (jnp.where(low, zero, k_roll), jnp.where(low, zero, v_roll)),
        (jnp.where(low, k_roll, zero), jnp.where(low, v_roll, zero)),
        (jnp.where(low, zero, k_a), jnp.where(low, zero, v_a)),
    )
    for n, (kk, vv) in enumerate(variants):
        kv_ref[n, BLOCK:2 * BLOCK, :] = kk.astype(BF16)
        kv_ref[4 + n, BLOCK:2 * BLOCK, :] = vv.astype(BF16)

    qi = lax.broadcasted_iota(jnp.int32, (BLOCK, 2 * BLOCK), 0)
    ki = lax.broadcasted_iota(jnp.int32, (BLOCK, 2 * BLOCK), 1)
    valid = (ki > qi) & (ki <= qi + WINDOW) & ((j > 0) | (ki >= BLOCK))
    neg_inf = jnp.full((BLOCK, 2 * BLOCK), -jnp.inf, F32)

    pairs = []
    for p in range(ATTN_HEADS // 2):
        kvh = p // 2
        qp = (q_a[:, p * LANES_V7X:(p + 1) * LANES_V7X] * (ATTN_HEAD_DIM ** -0.5)).astype(BF16)
        acc = None
        for half in range(2):
            s = _dot_nt(qp, kv_ref[2 * kvh + half])
            s = jnp.where(valid, s, neg_inf)
            sink = sink_ref[2 * p + half]
            m = jnp.maximum(jnp.max(s, axis=-1, keepdims=True), sink)
            e = jnp.exp(s - m)
            denom = jnp.sum(e, axis=-1, keepdims=True) + jnp.exp(sink - m)
            o = _dot(e.astype(BF16), kv_ref[4 + 2 * kvh + half]) * (1.0 / denom)
            acc = o if acc is None else acc + o
        pairs.append(acc)
    attn = jnp.concatenate(pairs, axis=-1)
    attn = (_rms_rows(attn) * attn_g_ref[...]).astype(BF16)

    kv_ref[:, 0:BLOCK, :] = kv_ref[:, BLOCK:2 * BLOCK, :]

    l0 = lb_logit_ref[0:1, :]
    l1 = lb_logit_ref[1:2, :]
    lmax = jnp.maximum(l0, l1)
    e0 = jnp.exp(l0 - lmax)
    lb = e0 / (e0 + jnp.exp(l1 - lmax))
    f = lb + (1.0 - lb) * jax.nn.sigmoid(f_h)
    kf = 1.0 - f
    g = jnp.log(f)
    g_hi, g_lo = _split_bf16(g)
    tri = tri_ref[...]
    b = _dot(tri, g_hi) + _dot(tri, g_lo)
    b_last = b[BLOCK - 1:BLOCK, :]
    qf = _silu(q_h)
    q_inter = qf * jnp.exp(b)
    k_state = kf * jnp.exp(b_last - b)
    state_decay = jnp.exp(b_last)
    roww = lax.broadcasted_iota(jnp.int32, (BLOCK, HGRN_WIDTH), 0)
    level_decay = [_level_decay(b, g, f, BLOCK >> (l + 1), roww) for l in range(N_LEVELS)]

    ti = lax.broadcasted_iota(jnp.int32, (BLOCK, BLOCK), 0)
    si = lax.broadcasted_iota(jnp.int32, (BLOCK, BLOCK), 1)
    diff = ti ^ si
    lower = ti > si
    level_mask = [((diff >> (N_LEVELS - 1 - l)) == 1) & lower for l in range(N_LEVELS)]
    row = lax.broadcasted_iota(jnp.int32, (BLOCK, HGRN_DIM), 0)
    upper = [(row & (BLOCK >> (l + 1))) != 0 for l in range(N_LEVELS)]

    recs = []
    for hh in range(HGRN_HEADS):
        cols = slice(hh * HGRN_DIM, (hh + 1) * HGRN_DIM)
        qh, kh, vh = qf[:, cols], kf[:, cols], i_h[:, cols]
        r = [(jnp.where(upper[l], qh, kh) * level_decay[l][:, cols]).astype(BF16) for l in range(N_LEVELS)]
        lhs = r + [qh.astype(BF16)]
        rhs = r + [kh.astype(BF16)]
        masks = level_mask + [ti == si]
        scores = jnp.zeros((BLOCK, BLOCK), F32)
        for n in range(0, N_LEVELS + 1, 2):
            s0, s1 = _pair_scores(lhs[n], rhs[n], lhs[n + 1], rhs[n + 1])
            scores = jnp.where(masks[n], s0, jnp.where(masks[n + 1], s1, scores))
        v_t = vh.T.astype(BF16)
        state_t = state_ref[hh]
        o = _dot_nt(jnp.concatenate([q_inter[:, cols].astype(BF16), scores.astype(BF16)], axis=1),
                    jnp.concatenate([state_t.astype(BF16), v_t], axis=1))
        state_ref[hh] = state_t * state_decay[:, cols] + _dot(v_t, k_state[:, cols].astype(BF16))
        o = _rms_rows(o) * hgrn_g_ref[:, cols] * _silu(g_h[:, cols])
        recs.append(o.astype(BF16))
    rec = jnp.concatenate(recs, axis=-1)

    mixed = _dot(attn, w_out_ref[0:ATTN_WIDTH, :]) + _dot(rec, w_out_ref[ATTN_WIDTH:, :])
    o_ref[0] = x + gate * mixed


def _mixer_call(x, mod, sinks, g_mix, w_in, b_in, attn_g, lb_logits, hgrn_g, w_out, tri):
    bsz, seq, _ = x.shape
    const = lambda b, j: (0, 0)
    return pl.pallas_call(
        _mixer_kernel,
        grid=(bsz, seq // BLOCK),
        in_specs=[
            pl.BlockSpec(memory_space=pltpu.SMEM),
            pl.BlockSpec((1, BLOCK, D_MODEL), lambda b, j: (b, j, 0)),
            pl.BlockSpec((1, N_MOD, D_MODEL), lambda b, j: (b, 0, 0)),
            pl.BlockSpec((1, D_MODEL), const),
            pl.BlockSpec((D_MODEL, IN_COLS), const),
            pl.BlockSpec((1, IN_COLS), const),
            pl.BlockSpec((1, ATTN_WIDTH), const),
            pl.BlockSpec((2, HGRN_WIDTH), const),
            pl.BlockSpec((1, HGRN_WIDTH), const),
            pl.BlockSpec((D_MODEL, D_MODEL), const),
            pl.BlockSpec((BLOCK, BLOCK), const),
        ],
        out_specs=pl.BlockSpec((1, BLOCK, D_MODEL), lambda b, j: (b, j, 0)),
        out_shape=jax.ShapeDtypeStruct(x.shape, F32),
        scratch_shapes=[
            pltpu.VMEM((8, 2 * BLOCK, LANES_V7X), BF16),
            pltpu.VMEM((HGRN_HEADS, HGRN_DIM, HGRN_DIM), F32),
        ],
        compiler_params=pltpu.CompilerParams(
            dimension_semantics=("arbitrary", "arbitrary"), vmem_limit_bytes=VMEM_LIMIT_BYTES),
        name="token_mixer",
    )(sinks, x, mod, g_mix, w_in, b_in, attn_g, lb_logits, hgrn_g, w_out, tri)


FFN_ROWS = 512
FFN_CHUNK = 256
CONV_HALO = 8


def _ffn_kernel(x_ref, mod_ref, g_ffn_ref, w_up_ref, conv_w_ref, conv_b_ref, w_down_ref, g_fin_ref,
                o_ref, carry_ref, ubuf_ref, act_ref):
    j = pl.program_id(1)

    @pl.when(j == 0)
    def _():
        carry_ref[...] = jnp.zeros_like(carry_ref)

    x = x_ref[0]
    shift = mod_ref[0, 3:4, :]
    scale = mod_ref[0, 4:5, :]
    gate = mod_ref[0, 5:6, :]
    h = (_rms_rows(x) * g_ffn_ref[...] * (1.0 + scale) + shift).astype(BF16)

    for c in range(D_FF // FFN_CHUNK):
        cols = slice(c * FFN_CHUNK, (c + 1) * FFN_CHUNK)
        u = _dot(h, w_up_ref[:, cols])
        v = _dot(h, w_up_ref[:, D_FF + c * FFN_CHUNK:D_FF + (c + 1) * FFN_CHUNK])
        ubuf_ref[0:CONV_HALO, :] = carry_ref[:, cols]
        ubuf_ref[CONV_HALO:, :] = u
        carry_ref[:, cols] = u[FFN_ROWS - CONV_HALO:, :]
        u1 = ubuf_ref[CONV_HALO - 1:CONV_HALO - 1 + FFN_ROWS, :]
        u2 = ubuf_ref[CONV_HALO - 2:CONV_HALO - 2 + FFN_ROWS, :]
        conv = (conv_w_ref[0:1, cols] * u2 + conv_w_ref[1:2, cols] * u1 + conv_w_ref[2:3, cols] * u
                + conv_b_ref[:, cols])
        act_ref[:, cols] = (_silu(conv) * v).astype(BF16)

    y = x + gate * _dot(act_ref[...], w_down_ref[...])
    o_ref[0] = _rms_rows(y) * g_fin_ref[...]


def _ffn_call(x1, mod, g_ffn, w_up, conv_w, conv_b, w_down, g_fin):
    bsz, seq, _ = x1.shape
    const = lambda b, j: (0, 0)
    return pl.pallas_call(
        _ffn_kernel,
        grid=(bsz, seq // FFN_ROWS),
        in_specs=[
            pl.BlockSpec((1, FFN_ROWS, D_MODEL), lambda b, j: (b, j, 0)),
            pl.BlockSpec((1, N_MOD, D_MODEL), lambda b, j: (b, 0, 0)),
            pl.BlockSpec((1, D_MODEL), const),
            pl.BlockSpec((D_MODEL, 2 * D_FF), const),
            pl.BlockSpec((3, D_FF), const),
            pl.BlockSpec((1, D_FF), const),
            pl.BlockSpec((D_FF, D_MODEL), const),
            pl.BlockSpec((1, D_MODEL), const),
        ],
        out_specs=pl.BlockSpec((1, FFN_ROWS, D_MODEL), lambda b, j: (b, j, 0)),
        out_shape=jax.ShapeDtypeStruct(x1.shape, F32),
        scratch_shapes=[
            pltpu.VMEM((CONV_HALO, D_FF), F32),
            pltpu.VMEM((CONV_HALO + FFN_ROWS, FFN_CHUNK), F32),
            pltpu.VMEM((FFN_ROWS, D_FF), BF16),
        ],
        compiler_params=pltpu.CompilerParams(
            dimension_semantics=("arbitrary", "arbitrary"), vmem_limit_bytes=VMEM_LIMIT_BYTES),
        name="channel_mixer",
    )(x1, mod, g_ffn, w_up, conv_w, conv_b, w_down, g_fin)


def kernel(x, c, ada_w, ada_b, mix_norm_g, w_in, b_in, attn_sinks, attn_out_g, hgrn_lb_logits, hgrn_out_g,
           w_out, ffn_norm_g, w_up, conv_w, conv_b, w_down, final_norm_g):
    assert ada_w.shape[0] == 1 and x.shape[-1] == D_MODEL and x.shape[1] % FFN_ROWS == 0
    bsz = x.shape[0]
    mod = _ada_call(c, ada_w[0], ada_b[0][None, :]).reshape(bsz, N_MOD, D_MODEL)
    tri = jnp.asarray(_cumsum_matrix(), dtype=BF16)
    x1 = _mixer_call(
        x, mod, attn_sinks[0], mix_norm_g[0][None, :], w_in[0].astype(BF16), b_in[0][None, :],
        attn_out_g[0][None, :], hgrn_lb_logits, hgrn_out_g[0][None, :], w_out[0].astype(BF16), tri)
    return _ffn_call(
        x1, mod, ffn_norm_g[0][None, :], w_up[0].astype(BF16), conv_w[0], conv_b[0][None, :],
        w_down[0].astype(BF16), final_norm_g[None, :])
```

```python
import numpy as np
import jax
import jax.numpy as jnp
from jax import lax
from jax.experimental import pallas as pl
from jax.experimental.pallas import tpu as pltpu

D_MODEL = 1024
ATTN_HEAD_DIM = 64
ATTN_WIDTH = 512
ATTN_HEADS = 8
ATTN_KV_HEADS = 2
ATTN_KV_WIDTH = 128
WINDOW = 128
HGRN_HEADS = 4
HGRN_DIM = 128
HGRN_WIDTH = 512
IN_SPLITS = (ATTN_WIDTH, ATTN_KV_WIDTH, ATTN_KV_WIDTH, HGRN_WIDTH, HGRN_WIDTH, HGRN_WIDTH, HGRN_WIDTH)
IN_OFFSETS = tuple(int(s) for s in np.cumsum((0,) + IN_SPLITS))
IN_COLS = IN_OFFSETS[-1]
D_FF = 2816
N_MOD = 6
EPS = 1e-6

BLOCK = 128
N_LEVELS = 7
LANES_V7X = 128
VMEM_LIMIT_BYTES = 56 * 1024 * 1024

F32 = jnp.float32
BF16 = jnp.bfloat16


def _dot(a, b):
    return jnp.dot(a, b, preferred_element_type=F32)


def _dot_nt(a, b):
    return lax.dot_general(a, b, (((1,), (1,)), ((), ())), preferred_element_type=F32)


def _split_bf16(v):
    hi = v.astype(BF16)
    lo = (v - hi.astype(F32)).astype(BF16)
    return hi, lo


def _rms_rows(v):
    return v * lax.rsqrt(jnp.mean(v * v, axis=-1, keepdims=True) + EPS)


def _silu(v):
    return v * jax.nn.sigmoid(v)


def _cumsum_matrix():
    t = np.arange(BLOCK)[:, None]
    r = np.arange(BLOCK)[None, :]
    return (r <= t).astype(np.float32)


def _rows_by_half(upper_src, lower_src, half):
    parts = [(upper_src if (start & half) else lower_src)[start:start + half] for start in range(0, BLOCK, half)]
    return jnp.concatenate(parts, axis=0)


def _level_decay(b, g, f, half, row):
    if half == 1:
        return jnp.where((row & 1) != 0, f, 1.0)
    if half == 2:
        up1 = pltpu.roll(g, BLOCK - 1, 0)
        dn1 = pltpu.roll(g, 1, 0)
        pos = row & 3
        d = jnp.where(pos == 0, up1, jnp.where(pos == 1, 0.0, jnp.where(pos == 2, g, g + dn1)))
        return jnp.exp(d)
    if half == 4:
        w = b.shape[-1]
        b3 = b.reshape(BLOCK // 8, 8, w)
        beta = jnp.broadcast_to(b3[:, 3:4, :], b3.shape).reshape(BLOCK, w)
        return jnp.exp(-jnp.abs(b - beta))
    parts = []
    for start in range(0, BLOCK, half):
        if start & half:
            ref = start - 1
            parts.append(b[start:start + half] - b[ref:ref + 1])
        else:
            ref = start + half - 1
            parts.append(b[ref:ref + 1] - b[start:start + half])
    return jnp.exp(jnp.concatenate(parts, axis=0))


def _pair_scores(a0, b0, a1, b1):
    z = jnp.zeros_like(b0)
    lhs = jnp.concatenate([a0, a1], axis=1)
    rhs = jnp.concatenate([jnp.concatenate([b0, z], axis=1), jnp.concatenate([z, b1], axis=1)], axis=0)
    out = _dot_nt(lhs, rhs)
    return out[:, :BLOCK], out[:, BLOCK:]


def _ada_kernel(c_ref, w_ref, b_ref, o_ref):
    cond = _silu(c_ref[...])
    c_hi, c_lo = _split_bf16(cond)
    w_hi, w_lo = _split_bf16(w_ref[...])
    o_ref[...] = _dot(c_hi, w_hi) + _dot(c_lo, w_hi) + _dot(c_hi, w_lo) + b_ref[...]


def _ada_call(c, w, b):
    bsz = c.shape[0]
    n = w.shape[1]
    tn = 512
    return pl.pallas_call(
        _ada_kernel,
        grid=(n // tn,),
        in_specs=[
            pl.BlockSpec((bsz, D_MODEL), lambda j: (0, 0)),
            pl.BlockSpec((D_MODEL, tn), lambda j: (0, j)),
            pl.BlockSpec((1, tn), lambda j: (0, j)),
        ],
        out_specs=pl.BlockSpec((bsz, tn), lambda j: (0, j)),
        out_shape=jax.ShapeDtypeStruct((bsz, n), F32),
        name="ada_mod",
    )(c, w, b)


PROJ_CHUNK = 256
MIX_SEQS = 2
KV_VARIANTS = 8


def _mixer_kernel(sink_ref, x_ref, xres_ref, mod_ref, g_mix_ref, w_in_ref, b_in_ref, attn_g_ref, lb_logit_ref,
                  hgrn_g_ref, w_out_ref, tri_ref, o_ref, kv_ref, state_ref, h_ref, proj_ref):
    j = pl.program_id(1)

    def reset_carry():
        kv_ref[:, 0:BLOCK, :] = jnp.zeros((MIX_SEQS * KV_VARIANTS, BLOCK, LANES_V7X), BF16)
        state_ref[...] = jnp.zeros_like(state_ref)

    @pl.when(j == 0)
    def _():
        reset_carry()
        for s in range(MIX_SEQS):
            h_ref[2 * s + 1] = jnp.zeros((BLOCK, D_MODEL), BF16)
            proj_ref[2 * s] = jnp.zeros((BLOCK, IN_COLS), F32)

    @pl.when(j % 2 == 0)
    def _():
        _mixer_step(j, 0, sink_ref, x_ref, xres_ref, mod_ref, g_mix_ref, w_in_ref, b_in_ref, attn_g_ref,
                    lb_logit_ref, hgrn_g_ref, w_out_ref, tri_ref, o_ref, kv_ref, state_ref, h_ref, proj_ref)

    @pl.when(j % 2 == 1)
    def _():
        _mixer_step(j, 1, sink_ref, x_ref, xres_ref, mod_ref, g_mix_ref, w_in_ref, b_in_ref, attn_g_ref,
                    lb_logit_ref, hgrn_g_ref, w_out_ref, tri_ref, o_ref, kv_ref, state_ref, h_ref, proj_ref)

    @pl.when(j == 1)
    def _():
        reset_carry()


def _mixer_step(j, parity, sink_ref, x_ref, xres_ref, mod_ref, g_mix_ref, w_in_ref, b_in_ref, attn_g_ref,
                lb_logit_ref, hgrn_g_ref, w_out_ref, tri_ref, o_ref, kv_ref, state_ref, h_ref, proj_ref):
    seqs = range(MIX_SEQS)
    mix_slot = parity
    proj_slot = 1 - parity
    norm_slot = parity

    shift = [mod_ref[s, 0:1, :] for s in seqs]
    gate = [mod_ref[s, 2:3, :] for s in seqs]
    norm_gain = [g_mix_ref[...] * (1.0 + mod_ref[s, 1:2, :]) for s in seqs]

    l0 = lb_logit_ref[0:1, :]
    l1 = lb_logit_ref[1:2, :]
    lmax = jnp.maximum(l0, l1)
    e0 = jnp.exp(l0 - lmax)
    lb = e0 / (e0 + jnp.exp(l1 - lmax))
    tri = tri_ref[...]

    lane = lax.broadcasted_iota(jnp.int32, (BLOCK, LANES_V7X), 1)
    low = lane < ATTN_HEAD_DIM
    qi = lax.broadcasted_iota(jnp.int32, (BLOCK, 2 * BLOCK), 0)
    ki = lax.broadcasted_iota(jnp.int32, (BLOCK, 2 * BLOCK), 1)
    band = (ki > qi) & (ki <= qi + WINDOW)
    bias = jnp.where(band & ((j > 2) | (ki >= BLOCK)), 0.0, -jnp.inf)

    ti = lax.broadcasted_iota(jnp.int32, (BLOCK, BLOCK), 0)
    si = lax.broadcasted_iota(jnp.int32, (BLOCK, BLOCK), 1)
    diff = ti ^ si
    lower = ti > si
    score_masks = [((diff >> (N_LEVELS - 1 - l)) == 1) & lower for l in range(N_LEVELS)] + [ti == si]
    roww = lax.broadcasted_iota(jnp.int32, (BLOCK, HGRN_WIDTH), 0)

    pending = [(s, c) for c in range(IN_COLS // PROJ_CHUNK) for s in seqs]

    def fill(count):
        for _ in range(min(count, len(pending))):
            s, c = pending.pop(0)
            cols = slice(c * PROJ_CHUNK, (c + 1) * PROJ_CHUNK)
            proj_ref[2 * s + proj_slot, :, cols] = _dot(h_ref[2 * s + proj_slot], w_in_ref[:, cols])

    def load(s, n):
        seg = slice(IN_OFFSETS[n], IN_OFFSETS[n + 1])
        return proj_ref[2 * s + mix_slot, :, seg] + b_in_ref[:, seg]

    fill(2)
    f, kf, g, b = [], [], [], []
    for s in seqs:
        f.append(lb + (1.0 - lb) * jax.nn.sigmoid(load(s, 4)))
        kf.append(1.0 - f[s])
        g.append(jnp.log(f[s]))
        g_hi, g_lo = _split_bf16(g[s])
        b.append(_dot(tri, g_hi) + _dot(tri, g_lo))

    logits = []
    for s in seqs:
        q_a, k_a, v_a = load(s, 0), load(s, 1), load(s, 2)
        zero = jnp.zeros((BLOCK, LANES_V7X), F32)
        k_roll = pltpu.roll(k_a, ATTN_HEAD_DIM, 1)
        v_roll = pltpu.roll(v_a, ATTN_HEAD_DIM, 1)
        variants = (
            (jnp.where(low, k_a, zero), jnp.where(low, v_a, zero)),
            (jnp.where(low, zero, k_roll), jnp.where(low, zero, v_roll)),
            (jnp.where(low, k_roll, zero), jnp.where(low, v_roll, zero)),
            (jnp.where(low, zero, k_a), jnp.where(low, zero, v_a)),
        )
        for n, (kk, vv) in enumerate(variants):
            kv_ref[KV_VARIANTS * s + n, BLOCK:2 * BLOCK, :] = kk.astype(BF16)
            kv_ref[KV_VARIANTS * s + 4 + n, BLOCK:2 * BLOCK, :] = vv.astype(BF16)
        per_pair = []
        for p in range(ATTN_HEADS // 2):
            qp = (q_a[:, p * LANES_V7X:(p + 1) * LANES_V7X] * (ATTN_HEAD_DIM ** -0.5)).astype(BF16)
            per_pair.append([_dot_nt(qp, kv_ref[KV_VARIANTS * s + 2 * (p // 2) + half]) + bias
                             for half in range(2)])
        logits.append(per_pair)
        fill(1)
    q_inter, k_state, state_decay, lhs, rhs, gated, i_h = [], [], [], [], [], [], []
    for s in seqs:
        q_h = load(s, 3)
        i_h.append(load(s, 5))
        gated.append(_silu(load(s, 6)))
        b_last = b[s][BLOCK - 1:BLOCK, :]
        qf = _silu(q_h)
        q_inter.append((qf * jnp.exp(b[s])).astype(BF16))
        k_state.append((kf[s] * jnp.exp(b_last - b[s])).astype(BF16))
        state_decay.append(jnp.exp(b_last))
        operands = []
        for l in range(N_LEVELS):
            half = BLOCK >> (l + 1)
            if half >= 8:
                src = _rows_by_half(qf, kf[s], half)
            else:
                src = jnp.where((roww & half) != 0, qf, kf[s])
            operands.append((src * _level_decay(b[s], g[s], f[s], half, roww)).astype(BF16))
        lhs.append(operands + [qf.astype(BF16)])
        rhs.append(operands + [kf[s].astype(BF16)])
        fill(2)

    pairs = [[] for _ in seqs]
    scores = [[] for _ in seqs]
    for p in range(ATTN_HEADS // 2):
        cols = slice(p * HGRN_DIM, (p + 1) * HGRN_DIM)
        for s in seqs:
            acc = None
            for half in range(2):
                sc = logits[s][p][half]
                sink = sink_ref[2 * p + half]
                m = jnp.maximum(jnp.max(sc, axis=-1, keepdims=True), sink)
                e = jnp.exp(sc - m)
                denom = jnp.sum(e, axis=-1, keepdims=True) + jnp.exp(sink - m)
                o = _dot(e.astype(BF16), kv_ref[KV_VARIANTS * s + 4 + 2 * (p // 2) + half]) * (1.0 / denom)
                acc = o if acc is None else acc + o
            pairs[s].append(acc)
            sc = jnp.zeros((BLOCK, BLOCK), F32)
            for n in range(0, N_LEVELS + 1, 2):
                s0, s1 = _pair_scores(lhs[s][n][:, cols], rhs[s][n][:, cols],
                                      lhs[s][n + 1][:, cols], rhs[s][n + 1][:, cols])
                sc = jnp.where(score_masks[n], s0, jnp.where(score_masks[n + 1], s1, sc))
            scores[s].append(sc.astype(BF16))
            fill(2)
    mixed = []
    for s in seqs:
        attn = jnp.concatenate(pairs[s], axis=-1)
        attn = (_rms_rows(attn) * attn_g_ref[...]).astype(BF16)
        mixed.append(_dot(attn, w_out_ref[0:ATTN_WIDTH, :]))
    for s in seqs:
        rows = slice(KV_VARIANTS * s, KV_VARIANTS * (s + 1))
        kv_ref[rows, 0:BLOCK, :] = kv_ref[rows, BLOCK:2 * BLOCK, :]

    recs = [[] for _ in seqs]
    for hh in range(HGRN_HEADS):
        cols = slice(hh * HGRN_DIM, (hh + 1) * HGRN_DIM)
        for s in seqs:
            v_t = i_h[s][:, cols].T.astype(BF16)
            state = state_ref[HGRN_HEADS * s + hh]
            o = _dot_nt(jnp.concatenate([q_inter[s][:, cols], scores[s][hh]], axis=1),
                        jnp.concatenate([state.astype(BF16), v_t], axis=1))
            state_ref[HGRN_HEADS * s + hh] = state * state_decay[s][:, cols] + _dot(v_t, k_state[s][:, cols])
            recs[s].append((_rms_rows(o) * hgrn_g_ref[:, cols] * gated[s][:, cols]).astype(BF16))
    fill(len(pending))

    for s in seqs:
        out = mixed[s] + _dot(jnp.concatenate(recs[s], axis=-1), w_out_ref[ATTN_WIDTH:, :])
        o_ref[s] = xres_ref[s] + gate[s] * out

    for s in seqs:
        h_ref[2 * s + norm_slot] = (_rms_rows(x_ref[s]) * norm_gain[s] + shift[s]).astype(BF16)


def _mixer_call(x, mod, sinks, g_mix, w_in, b_in, attn_g, lb_logits, hgrn_g, w_out, tri):
    bsz, seq, _ = x.shape
    n_blocks = seq // BLOCK
    const = lambda b, j: (0, 0)
    return pl.pallas_call(
        _mixer_kernel,
        grid=(bsz // MIX_SEQS, n_blocks + 2),
        in_specs=[
            pl.BlockSpec(memory_space=pltpu.SMEM),
            pl.BlockSpec((MIX_SEQS, BLOCK, D_MODEL), lambda b, j: (b, jnp.minimum(j, n_blocks - 1), 0)),
            pl.BlockSpec((MIX_SEQS, BLOCK, D_MODEL), lambda b, j: (b, jnp.maximum(j - 2, 0), 0)),
            pl.BlockSpec((MIX_SEQS, N_MOD, D_MODEL), lambda b, j: (b, 0, 0)),
            pl.BlockSpec((1, D_MODEL), const),
            pl.BlockSpec((D_MODEL, IN_COLS), const),
            pl.BlockSpec((1, IN_COLS), const),
            pl.BlockSpec((1, ATTN_WIDTH), const),
            pl.BlockSpec((2, HGRN_WIDTH), const),
            pl.BlockSpec((1, HGRN_WIDTH), const),
            pl.BlockSpec((D_MODEL, D_MODEL), const),
            pl.BlockSpec((BLOCK, BLOCK), const),
        ],
        out_specs=pl.BlockSpec((MIX_SEQS, BLOCK, D_MODEL), lambda b, j: (b, jnp.maximum(j - 2, 0), 0)),
        out_shape=jax.ShapeDtypeStruct(x.shape, F32),
        scratch_shapes=[
            pltpu.VMEM((MIX_SEQS * KV_VARIANTS, 2 * BLOCK, LANES_V7X), BF16),
            pltpu.VMEM((MIX_SEQS * HGRN_HEADS, HGRN_DIM, HGRN_DIM), F32),
            pltpu.VMEM((MIX_SEQS * 2, BLOCK, D_MODEL), BF16),
            pltpu.VMEM((MIX_SEQS * 2, BLOCK, IN_COLS), F32),
        ],
        compiler_params=pltpu.CompilerParams(
            dimension_semantics=("arbitrary", "arbitrary"), vmem_limit_bytes=VMEM_LIMIT_BYTES),
        name="token_mixer",
    )(sinks, x, x, mod, g_mix, w_in, b_in, attn_g, lb_logits, hgrn_g, w_out, tri)


FFN_ROWS = 1024
FFN_CHUNK = 256
CONV_HALO = 8


def _ffn_kernel(x_ref, mod_ref, g_ffn_ref, w_up_ref, conv_w_ref, conv_b_ref, w_down_ref, g_fin_ref,
                o_ref, carry_ref, ubuf_ref, act_ref):
    j = pl.program_id(1)

    @pl.when(j == 0)
    def _():
        carry_ref[...] = jnp.zeros_like(carry_ref)

    x = x_ref[0]
    shift = mod_ref[0, 3:4, :]
    scale = mod_ref[0, 4:5, :]
    gate = mod_ref[0, 5:6, :]
    h = (_rms_rows(x) * g_ffn_ref[...] * (1.0 + scale) + shift).astype(BF16)

    for c in range(D_FF // FFN_CHUNK):
        cols = slice(c * FFN_CHUNK, (c + 1) * FFN_CHUNK)
        u = _dot(h, w_up_ref[:, cols])
        v = _dot(h, w_up_ref[:, D_FF + c * FFN_CHUNK:D_FF + (c + 1) * FFN_CHUNK])
        ubuf_ref[0:CONV_HALO, :] = carry_ref[:, cols]
        ubuf_ref[CONV_HALO:, :] = u
        carry_ref[:, cols] = u[FFN_ROWS - CONV_HALO:, :]
        u1 = ubuf_ref[CONV_HALO - 1:CONV_HALO - 1 + FFN_ROWS, :]
        u2 = ubuf_ref[CONV_HALO - 2:CONV_HALO - 2 + FFN_ROWS, :]
        conv = (conv_w_ref[0:1, cols] * u2 + conv_w_ref[1:2, cols] * u1 + conv_w_ref[2:3, cols] * u
                + conv_b_ref[:, cols])
        act_ref[:, cols] = (_silu(conv) * v).astype(BF16)

    y = x + gate * _dot(act_ref[...], w_down_ref[...])
    o_ref[0] = _rms_rows(y) * g_fin_ref[...]


def _ffn_call(x1, mod, g_ffn, w_up, conv_w, conv_b, w_down, g_fin):
    bsz, seq, _ = x1.shape
    const = lambda b, j: (0, 0)
    return pl.pallas_call(
        _ffn_kernel,
        grid=(bsz, seq // FFN_ROWS),
        in_specs=[
            pl.BlockSpec((1, FFN_ROWS, D_MODEL), lambda b, j: (b, j, 0)),
            pl.BlockSpec((1, N_MOD, D_MODEL), lambda b, j: (b, 0, 0)),
            pl.BlockSpec((1, D_MODEL), const),
            pl.BlockSpec((D_MODEL, 2 * D_FF), const, pipeline_mode=pl.Buffered(1)),
            pl.BlockSpec((3, D_FF), const),
            pl.BlockSpec((1, D_FF), const),
            pl.BlockSpec((D_FF, D_MODEL), const, pipeline_mode=pl.Buffered(1)),
            pl.BlockSpec((1, D_MODEL), const),
        ],
        out_specs=pl.BlockSpec((1, FFN_ROWS, D_MODEL), lambda b, j: (b, j, 0)),
        out_shape=jax.ShapeDtypeStruct(x1.shape, F32),
        scratch_shapes=[
            pltpu.VMEM((CONV_HALO, D_FF), F32),
            pltpu.VMEM((CONV_HALO + FFN_ROWS, FFN_CHUNK), F32),
            pltpu.VMEM((FFN_ROWS, D_FF), BF16),
        ],
        compiler_params=pltpu.CompilerParams(
            dimension_semantics=("arbitrary", "arbitrary"), vmem_limit_bytes=VMEM_LIMIT_BYTES),
        name="channel_mixer",
    )(x1, mod, g_ffn, w_up, conv_w, conv_b, w_down, g_fin)


def kernel(x, c, ada_w, ada_b, mix_norm_g, w_in, b_in, attn_sinks, attn_out_g, hgrn_lb_logits, hgrn_out_g,
           w_out, ffn_norm_g, w_up, conv_w, conv_b, w_down, final_norm_g):
    assert ada_w.shape[0] == 1 and x.shape[-1] == D_MODEL and x.shape[1] % FFN_ROWS == 0
    assert x.shape[0] % MIX_SEQS == 0
    bsz = x.shape[0]
    mod = _ada_call(c, ada_w[0], ada_b[0][None, :]).reshape(bsz, N_MOD, D_MODEL)
    tri = jnp.asarray(_cumsum_matrix(), dtype=BF16)
    x1 = _mixer_call(
        x, mod, attn_sinks[0], mix_norm_g[0][None, :], w_in[0].astype(BF16), b_in[0][None, :],
        attn_out_g[0][None, :], hgrn_lb_logits, hgrn_out_g[0][None, :], w_out[0].astype(BF16), tri)
    return _ffn_call(
        x1, mod, ffn_norm_g[0][None, :], w_up[0].astype(BF16), conv_w[0], conv_b[0][None, :],
        w_down[0].astype(BF16), final_norm_g[None, :])
```

```python
import numpy as np
import jax
import jax.numpy as jnp
from jax import lax
from jax.experimental import pallas as pl
from jax.experimental.pallas import tpu as pltpu

D_MODEL = 1024
ATTN_HEAD_DIM = 64
ATTN_WIDTH = 512
ATTN_HEADS = 8
ATTN_KV_HEADS = 2
ATTN_KV_WIDTH = 128
WINDOW = 128
HGRN_HEADS = 4
HGRN_DIM = 128
HGRN_WIDTH = 512
IN_SPLITS = (ATTN_WIDTH, ATTN_KV_WIDTH, ATTN_KV_WIDTH, HGRN_WIDTH, HGRN_WIDTH, HGRN_WIDTH, HGRN_WIDTH)
IN_OFFSETS = tuple(int(s) for s in np.cumsum((0,) + IN_SPLITS))
IN_COLS = IN_OFFSETS[-1]
D_FF = 2816
N_MOD = 6
EPS = 1e-6
LOG2_E = 1.4426950408889634

BLOCK = 128
N_LEVELS = 7
LANES_V7X = 128
VMEM_LIMIT_BYTES = 56 * 1024 * 1024

F32 = jnp.float32
BF16 = jnp.bfloat16


def _dot(a, b):
    return jnp.dot(a, b, preferred_element_type=F32)


def _dot_nt(a, b):
    return lax.dot_general(a, b, (((1,), (1,)), ((), ())), preferred_element_type=F32)


def _split_bf16(v):
    hi = v.astype(BF16)
    lo = (v - hi.astype(F32)).astype(BF16)
    return hi, lo


def _rms_rows(v):
    return v * lax.rsqrt(jnp.mean(v * v, axis=-1, keepdims=True) + EPS)


def _silu(v):
    return v * jax.nn.sigmoid(v)


def _cumsum_matrix():
    t = np.arange(BLOCK)[:, None]
    r = np.arange(BLOCK)[None, :]
    return (r <= t).astype(np.float32)


def _rows_by_half(upper_src, lower_src, half):
    parts = [(upper_src if (start & half) else lower_src)[start:start + half] for start in range(0, BLOCK, half)]
    return jnp.concatenate(parts, axis=0)


def _level_decay(b, g, f, half, row):
    if half == 1:
        return jnp.where((row & 1) != 0, f, 1.0)
    if half == 2:
        up1 = pltpu.roll(g, BLOCK - 1, 0)
        dn1 = pltpu.roll(g, 1, 0)
        pos = row & 3
        d = jnp.where(pos == 0, up1, jnp.where(pos == 1, 0.0, jnp.where(pos == 2, g, g + dn1)))
        return jnp.exp(d)
    if half == 4:
        w = b.shape[-1]
        b3 = b.reshape(BLOCK // 8, 8, w)
        beta = jnp.broadcast_to(b3[:, 3:4, :], b3.shape).reshape(BLOCK, w)
        return jnp.exp(-jnp.abs(b - beta))
    parts = []
    for start in range(0, BLOCK, half):
        if start & half:
            ref = start - 1
            parts.append(b[start:start + half] - b[ref:ref + 1])
        else:
            ref = start + half - 1
            parts.append(b[ref:ref + 1] - b[start:start + half])
    return jnp.exp(jnp.concatenate(parts, axis=0))


def _pair_scores(a0, b0, a1, b1):
    z = jnp.zeros_like(b0)
    lhs = jnp.concatenate([a0, a1], axis=1)
    rhs = jnp.concatenate([jnp.concatenate([b0, z], axis=1), jnp.concatenate([z, b1], axis=1)], axis=0)
    out = _dot_nt(lhs, rhs)
    return out[:, :BLOCK], out[:, BLOCK:]


def _ada_kernel(c_ref, w_ref, b_ref, o_ref):
    cond = _silu(c_ref[...])
    c_hi, c_lo = _split_bf16(cond)
    w_hi, w_lo = _split_bf16(w_ref[...])
    o_ref[...] = _dot(c_hi, w_hi) + _dot(c_lo, w_hi) + _dot(c_hi, w_lo) + b_ref[...]


def _ada_call(c, w, b):
    bsz = c.shape[0]
    n = w.shape[1]
    tn = 512
    return pl.pallas_call(
        _ada_kernel,
        grid=(n // tn,),
        in_specs=[
            pl.BlockSpec((bsz, D_MODEL), lambda j: (0, 0)),
            pl.BlockSpec((D_MODEL, tn), lambda j: (0, j)),
            pl.BlockSpec((1, tn), lambda j: (0, j)),
        ],
        out_specs=pl.BlockSpec((bsz, tn), lambda j: (0, j)),
        out_shape=jax.ShapeDtypeStruct((bsz, n), F32),
        name="ada_mod",
    )(c, w, b)


PROJ_CHUNK = 256
MIX_SEQS = 2
KV_VARIANTS = 8


def _mixer_kernel(sink_ref, x_ref, xres_ref, mod_ref, g_mix_ref, w_in_ref, b_in_ref, attn_g_ref, lb_logit_ref,
                  hgrn_g_ref, w_out_ref, tri_ref, o_ref, kv_ref, state_ref, h_ref, proj_ref):
    j = pl.program_id(1)

    def reset_carry():
        kv_ref[:, 0:BLOCK, :] = jnp.zeros((MIX_SEQS * KV_VARIANTS, BLOCK, LANES_V7X), BF16)
        state_ref[...] = jnp.zeros_like(state_ref)

    @pl.when(j == 0)
    def _():
        reset_carry()
        h_ref[1] = jnp.zeros((MIX_SEQS * BLOCK, D_MODEL), BF16)
        proj_ref[0] = jnp.zeros((MIX_SEQS * BLOCK, IN_COLS), F32)

    @pl.when(j % 2 == 0)
    def _():
        _mixer_step(j, 0, sink_ref, x_ref, xres_ref, mod_ref, g_mix_ref, w_in_ref, b_in_ref, attn_g_ref,
                    lb_logit_ref, hgrn_g_ref, w_out_ref, tri_ref, o_ref, kv_ref, state_ref, h_ref, proj_ref)

    @pl.when(j % 2 == 1)
    def _():
        _mixer_step(j, 1, sink_ref, x_ref, xres_ref, mod_ref, g_mix_ref, w_in_ref, b_in_ref, attn_g_ref,
                    lb_logit_ref, hgrn_g_ref, w_out_ref, tri_ref, o_ref, kv_ref, state_ref, h_ref, proj_ref)

    @pl.when(j == 1)
    def _():
        reset_carry()


def _mixer_step(j, parity, sink_ref, x_ref, xres_ref, mod_ref, g_mix_ref, w_in_ref, b_in_ref, attn_g_ref,
                lb_logit_ref, hgrn_g_ref, w_out_ref, tri_ref, o_ref, kv_ref, state_ref, h_ref, proj_ref):
    seqs = range(MIX_SEQS)
    mix_slot = parity
    proj_slot = 1 - parity
    norm_slot = parity

    shift = [mod_ref[s, 0:1, :] for s in seqs]
    gate = [mod_ref[s, 2:3, :] for s in seqs]
    norm_gain = [g_mix_ref[...] * (1.0 + mod_ref[s, 1:2, :]) for s in seqs]

    l0 = lb_logit_ref[0:1, :]
    l1 = lb_logit_ref[1:2, :]
    lmax = jnp.maximum(l0, l1)
    e0 = jnp.exp(l0 - lmax)
    lb = e0 / (e0 + jnp.exp(l1 - lmax))
    tri = tri_ref[...]

    lane = lax.broadcasted_iota(jnp.int32, (BLOCK, LANES_V7X), 1)
    low = lane < ATTN_HEAD_DIM
    qi = lax.broadcasted_iota(jnp.int32, (BLOCK, 2 * BLOCK), 0)
    ki = lax.broadcasted_iota(jnp.int32, (BLOCK, 2 * BLOCK), 1)
    band = (ki > qi) & (ki <= qi + WINDOW)
    bias = jnp.where(band & ((j > 2) | (ki >= BLOCK)), 0.0, -jnp.inf)
    bias2 = jnp.concatenate([bias, bias], axis=0)

    ti = lax.broadcasted_iota(jnp.int32, (BLOCK, BLOCK), 0)
    si = lax.broadcasted_iota(jnp.int32, (BLOCK, BLOCK), 1)
    diff = ti ^ si
    lower = ti > si
    score_masks = [((diff >> (N_LEVELS - 1 - l)) == 1) & lower for l in range(N_LEVELS)] + [ti == si]
    roww = lax.broadcasted_iota(jnp.int32, (BLOCK, HGRN_WIDTH), 0)

    pending = list(range(IN_COLS // PROJ_CHUNK))

    def fill(count):
        for _ in range(min(count, len(pending))):
            c = pending.pop(0)
            cols = slice(c * PROJ_CHUNK, (c + 1) * PROJ_CHUNK)
            proj_ref[proj_slot, :, cols] = _dot(h_ref[proj_slot], w_in_ref[:, cols])

    def load(s, n):
        seg = slice(IN_OFFSETS[n], IN_OFFSETS[n + 1])
        return proj_ref[mix_slot, s * BLOCK:(s + 1) * BLOCK, seg] + b_in_ref[:, seg]

    fill(1)
    f, kf, g, b = [], [], [], []
    for s in seqs:
        f.append(lb + (1.0 - lb) * jax.nn.sigmoid(load(s, 4)))
        kf.append(1.0 - f[s])
        g.append(jnp.log(f[s]))
        g_hi, g_lo = _split_bf16(g[s])
        b.append(_dot(tri, g_hi) + _dot(tri, g_lo))

    logits = []
    for s in seqs:
        q_a, k_a, v_a = load(s, 0), load(s, 1), load(s, 2)
        zero = jnp.zeros((BLOCK, LANES_V7X), F32)
        k_roll = pltpu.roll(k_a, ATTN_HEAD_DIM, 1)
        v_roll = pltpu.roll(v_a, ATTN_HEAD_DIM, 1)
        variants = (
            (jnp.where(low, k_a, zero), jnp.where(low, v_a, zero)),
            (jnp.where(low, zero, k_roll), jnp.where(low, zero, v_roll)),
            (jnp.where(low, k_roll, zero), jnp.where(low, v_roll, zero)),
            (jnp.where(low, zero, k_a), jnp.where(low, zero, v_a)),
        )
        for n, (kk, vv) in enumerate(variants):
            kv_ref[KV_VARIANTS * s + n, BLOCK:2 * BLOCK, :] = kk.astype(BF16)
            kv_ref[KV_VARIANTS * s + 4 + n, BLOCK:2 * BLOCK, :] = vv.astype(BF16)
        q_scaled = (q_a * (ATTN_HEAD_DIM ** -0.5 * LOG2_E)).astype(BF16)
        per_kv = []
        for kvh in range(ATTN_KV_HEADS):
            qq = jnp.concatenate([q_scaled[:, (2 * kvh + r) * LANES_V7X:(2 * kvh + r + 1) * LANES_V7X]
                                  for r in range(2)], axis=0)
            per_kv.append([_dot_nt(qq, kv_ref[KV_VARIANTS * s + 2 * kvh + half]) + bias2
                           for half in range(2)])
        logits.append(per_kv)
        fill(1)
    q_inter, k_state, state_decay, lhs, rhs, gated, i_h = [], [], [], [], [], [], []
    for s in seqs:
        q_h = load(s, 3)
        i_h.append(load(s, 5))
        gated.append(_silu(load(s, 6)))
        b_last = b[s][BLOCK - 1:BLOCK, :]
        qf = _silu(q_h)
        q_inter.append((qf * jnp.exp(b[s])).astype(BF16))
        k_state.append((kf[s] * jnp.exp(b_last - b[s])).astype(BF16))
        state_decay.append(jnp.exp(b_last))
        operands = []
        for l in range(N_LEVELS):
            half = BLOCK >> (l + 1)
            if half >= 8:
                src = _rows_by_half(qf, kf[s], half)
            else:
                src = jnp.where((roww & half) != 0, qf, kf[s])
            operands.append((src * _level_decay(b[s], g[s], f[s], half, roww)).astype(BF16))
        lhs.append(operands + [qf.astype(BF16)])
        rhs.append(operands + [kf[s].astype(BF16)])
        fill(1)

    pairs = [[] for _ in seqs]
    scores = [[] for _ in seqs]
    for kvh in range(ATTN_KV_HEADS):
        for s in seqs:
            acc = [None, None]
            for half in range(2):
                sc = logits[s][kvh][half]
                probs, inv = [], []
                for r in range(2):
                    sink = sink_ref[2 * (2 * kvh + r) + half] * LOG2_E
                    rows = sc[r * BLOCK:(r + 1) * BLOCK]
                    m = jnp.maximum(jnp.max(rows, axis=-1, keepdims=True), sink)
                    e = jnp.exp2(rows - m)
                    inv.append(1.0 / (jnp.sum(e, axis=-1, keepdims=True) + jnp.exp2(sink - m)))
                    probs.append(e.astype(BF16))
                o = _dot(jnp.concatenate(probs, axis=0), kv_ref[KV_VARIANTS * s + 4 + 2 * kvh + half])
                for r in range(2):
                    term = o[r * BLOCK:(r + 1) * BLOCK] * inv[r]
                    acc[r] = term if acc[r] is None else acc[r] + term
            pairs[s].extend(acc)
            for hh in (2 * kvh, 2 * kvh + 1):
                cols = slice(hh * HGRN_DIM, (hh + 1) * HGRN_DIM)
                sc = jnp.zeros((BLOCK, BLOCK), F32)
                for n in range(0, N_LEVELS + 1, 2):
                    s0, s1 = _pair_scores(lhs[s][n][:, cols], rhs[s][n][:, cols],
                                          lhs[s][n + 1][:, cols], rhs[s][n + 1][:, cols])
                    sc = jnp.where(score_masks[n], s0, jnp.where(score_masks[n + 1], s1, sc))
                scores[s].append(sc.astype(BF16))
                fill(1)
    attn = []
    for s in seqs:
        attn_s = jnp.concatenate(pairs[s], axis=-1)
        attn.append((_rms_rows(attn_s) * attn_g_ref[...]).astype(BF16))
    mixed = _dot(jnp.concatenate(attn, axis=0), w_out_ref[0:ATTN_WIDTH, :])
    for s in seqs:
        rows = slice(KV_VARIANTS * s, KV_VARIANTS * (s + 1))
        kv_ref[rows, 0:BLOCK, :] = kv_ref[rows, BLOCK:2 * BLOCK, :]

    recs = [[] for _ in seqs]
    for hh in range(HGRN_HEADS):
        cols = slice(hh * HGRN_DIM, (hh + 1) * HGRN_DIM)
        for s in seqs:
            v_t = i_h[s][:, cols].T.astype(BF16)
            state = state_ref[HGRN_HEADS * s + hh]
            o = _dot_nt(jnp.concatenate([q_inter[s][:, cols], scores[s][hh]], axis=1),
                        jnp.concatenate([state.astype(BF16), v_t], axis=1))
            state_ref[HGRN_HEADS * s + hh] = state * state_decay[s][:, cols] + _dot(v_t, k_state[s][:, cols])
            recs[s].append((_rms_rows(o) * hgrn_g_ref[:, cols] * gated[s][:, cols]).astype(BF16))
    fill(len(pending))

    rec = jnp.concatenate([jnp.concatenate(recs[s], axis=-1) for s in seqs], axis=0)
    mixed = mixed + _dot(rec, w_out_ref[ATTN_WIDTH:, :])
    for s in seqs:
        o_ref[s] = xres_ref[s] + gate[s] * mixed[s * BLOCK:(s + 1) * BLOCK]

    for s in seqs:
        h_ref[norm_slot, s * BLOCK:(s + 1) * BLOCK, :] = (_rms_rows(x_ref[s]) * norm_gain[s] + shift[s]).astype(BF16)


def _mixer_call(x, mod, sinks, g_mix, w_in, b_in, attn_g, lb_logits, hgrn_g, w_out, tri):
    bsz, seq, _ = x.shape
    n_blocks = seq // BLOCK
    const = lambda b, j: (0, 0)
    return pl.pallas_call(
        _mixer_kernel,
        grid=(bsz // MIX_SEQS, n_blocks + 2),
        in_specs=[
            pl.BlockSpec(memory_space=pltpu.SMEM),
            pl.BlockSpec((MIX_SEQS, BLOCK, D_MODEL), lambda b, j: (b, jnp.minimum(j, n_blocks - 1), 0)),
            pl.BlockSpec((MIX_SEQS, BLOCK, D_MODEL), lambda b, j: (b, jnp.maximum(j - 2, 0), 0)),
            pl.BlockSpec((MIX_SEQS, N_MOD, D_MODEL), lambda b, j: (b, 0, 0)),
            pl.BlockSpec((1, D_MODEL), const),
            pl.BlockSpec((D_MODEL, IN_COLS), const),
            pl.BlockSpec((1, IN_COLS), const),
            pl.BlockSpec((1, ATTN_WIDTH), const),
            pl.BlockSpec((2, HGRN_WIDTH), const),
            pl.BlockSpec((1, HGRN_WIDTH), const),
            pl.BlockSpec((D_MODEL, D_MODEL), const),
            pl.BlockSpec((BLOCK, BLOCK), const),
        ],
        out_specs=pl.BlockSpec((MIX_SEQS, BLOCK, D_MODEL), lambda b, j: (b, jnp.maximum(j - 2, 0), 0)),
        out_shape=jax.ShapeDtypeStruct(x.shape, F32),
        scratch_shapes=[
            pltpu.VMEM((MIX_SEQS * KV_VARIANTS, 2 * BLOCK, LANES_V7X), BF16),
            pltpu.VMEM((MIX_SEQS * HGRN_HEADS, HGRN_DIM, HGRN_DIM), F32),
            pltpu.VMEM((2, MIX_SEQS * BLOCK, D_MODEL), BF16),
            pltpu.VMEM((2, MIX_SEQS * BLOCK, IN_COLS), F32),
        ],
        compiler_params=pltpu.CompilerParams(
            dimension_semantics=("arbitrary", "arbitrary"), vmem_limit_bytes=VMEM_LIMIT_BYTES),
        name="token_mixer",
    )(sinks, x, x, mod, g_mix, w_in, b_in, attn_g, lb_logits, hgrn_g, w_out, tri)


FFN_ROWS = 1024
FFN_CHUNK = 256
CONV_HALO = 8


def _ffn_kernel(x_ref, mod_ref, g_ffn_ref, w_up_ref, conv_w_ref, conv_b_ref, w_down_ref, g_fin_ref,
                o_ref, carry_ref, ubuf_ref, act_ref):
    j = pl.program_id(1)

    @pl.when(j == 0)
    def _():
        carry_ref[...] = jnp.zeros_like(carry_ref)

    x = x_ref[0]
    shift = mod_ref[0, 3:4, :]
    scale = mod_ref[0, 4:5, :]
    gate = mod_ref[0, 5:6, :]
    h = (_rms_rows(x) * g_ffn_ref[...] * (1.0 + scale) + shift).astype(BF16)

    for c in range(D_FF // FFN_CHUNK):
        cols = slice(c * FFN_CHUNK, (c + 1) * FFN_CHUNK)
        u = _dot(h, w_up_ref[:, cols])
        v = _dot(h, w_up_ref[:, D_FF + c * FFN_CHUNK:D_FF + (c + 1) * FFN_CHUNK])
        ubuf_ref[0:CONV_HALO, :] = carry_ref[:, cols]
        ubuf_ref[CONV_HALO:, :] = u
        carry_ref[:, cols] = u[FFN_ROWS - CONV_HALO:, :]
        u1 = ubuf_ref[CONV_HALO - 1:CONV_HALO - 1 + FFN_ROWS, :]
        u2 = ubuf_ref[CONV_HALO - 2:CONV_HALO - 2 + FFN_ROWS, :]
        conv = (conv_w_ref[0:1, cols] * u2 + conv_w_ref[1:2, cols] * u1 + conv_w_ref[2:3, cols] * u
                + conv_b_ref[:, cols])
        act_ref[:, cols] = (_silu(conv) * v).astype(BF16)

    y = x + gate * _dot(act_ref[...], w_down_ref[...])
    o_ref[0] = _rms_rows(y) * g_fin_ref[...]


def _ffn_call(x1, mod, g_ffn, w_up, conv_w, conv_b, w_down, g_fin):
    bsz, seq, _ = x1.shape
    const = lambda b, j: (0, 0)
    return pl.pallas_call(
        _ffn_kernel,
        grid=(bsz, seq // FFN_ROWS),
        in_specs=[
            pl.BlockSpec((1, FFN_ROWS, D_MODEL), lambda b, j: (b, j, 0)),
            pl.BlockSpec((1, N_MOD, D_MODEL), lambda b, j: (b, 0, 0)),
            pl.BlockSpec((1, D_MODEL), const),
            pl.BlockSpec((D_MODEL, 2 * D_FF), const, pipeline_mode=pl.Buffered(1)),
            pl.BlockSpec((3, D_FF), const),
            pl.BlockSpec((1, D_FF), const),
            pl.BlockSpec((D_FF, D_MODEL), const, pipeline_mode=pl.Buffered(1)),
            pl.BlockSpec((1, D_MODEL), const),
        ],
        out_specs=pl.BlockSpec((1, FFN_ROWS, D_MODEL), lambda b, j: (b, j, 0)),
        out_shape=jax.ShapeDtypeStruct(x1.shape, F32),
        scratch_shapes=[
            pltpu.VMEM((CONV_HALO, D_FF), F32),
            pltpu.VMEM((CONV_HALO + FFN_ROWS, FFN_CHUNK), F32),
            pltpu.VMEM((FFN_ROWS, D_FF), BF16),
        ],
        compiler_params=pltpu.CompilerParams(
            dimension_semantics=("arbitrary", "arbitrary"), vmem_limit_bytes=VMEM_LIMIT_BYTES),
        name="channel_mixer",
    )(x1, mod, g_ffn, w_up, conv_w, conv_b, w_down, g_fin)


def kernel(x, c, ada_w, ada_b, mix_norm_g, w_in, b_in, attn_sinks, attn_out_g, hgrn_lb_logits, hgrn_out_g,
           w_out, ffn_norm_g, w_up, conv_w, conv_b, w_down, final_norm_g):
    assert ada_w.shape[0] == 1 and x.shape[-1] == D_MODEL and x.shape[1] % FFN_ROWS == 0
    assert x.shape[0] % MIX_SEQS == 0
    bsz = x.shape[0]
    mod = _ada_call(c, ada_w[0], ada_b[0][None, :]).reshape(bsz, N_MOD, D_MODEL)
    tri = jnp.asarray(_cumsum_matrix(), dtype=BF16)
    x1 = _mixer_call(
        x, mod, attn_sinks[0], mix_norm_g[0][None, :], w_in[0].astype(BF16), b_in[0][None, :],
        attn_out_g[0][None, :], hgrn_lb_logits, hgrn_out_g[0][None, :], w_out[0].astype(BF16), tri)
    return _ffn_call(
        x1, mod, ffn_norm_g[0][None, :], w_up[0].astype(BF16), conv_w[0], conv_b[0][None, :],
        w_down[0].astype(BF16), final_norm_g[None, :])
```

```python
import numpy as np
import jax
import jax.numpy as jnp
from jax import lax
from jax.experimental import pallas as pl
from jax.experimental.pallas import tpu as pltpu

D_MODEL = 1024
ATTN_HEAD_DIM = 64
ATTN_WIDTH = 512
ATTN_HEADS = 8
ATTN_KV_HEADS = 2
ATTN_KV_WIDTH = 128
WINDOW = 128
HGRN_HEADS = 4
HGRN_DIM = 128
HGRN_WIDTH = 512
IN_SPLITS = (ATTN_WIDTH, ATTN_KV_WIDTH, ATTN_KV_WIDTH, HGRN_WIDTH, HGRN_WIDTH, HGRN_WIDTH, HGRN_WIDTH)
IN_OFFSETS = tuple(int(s) for s in np.cumsum((0,) + IN_SPLITS))
IN_COLS = IN_OFFSETS[-1]
D_FF = 2816
N_MOD = 6
EPS = 1e-6
LOG2_E = 1.4426950408889634

BLOCK = 128
N_LEVELS = 7
LANES_V7X = 128
VMEM_LIMIT_BYTES = 56 * 1024 * 1024

F32 = jnp.float32
BF16 = jnp.bfloat16


def _dot(a, b):
    return jnp.dot(a, b, preferred_element_type=F32)


def _dot_nt(a, b):
    return lax.dot_general(a, b, (((1,), (1,)), ((), ())), preferred_element_type=F32)


def _split_bf16(v):
    hi = v.astype(BF16)
    lo = (v - hi.astype(F32)).astype(BF16)
    return hi, lo


def _rms_rows(v):
    return v * lax.rsqrt(jnp.mean(v * v, axis=-1, keepdims=True) + EPS)


def _silu(v):
    return v * jax.nn.sigmoid(v)


def _cumsum_matrix():
    t = np.arange(BLOCK)[:, None]
    r = np.arange(BLOCK)[None, :]
    return (r <= t).astype(np.float32)


def _rows_by_half(upper_src, lower_src, half):
    parts = [(upper_src if (start & half) else lower_src)[start:start + half] for start in range(0, BLOCK, half)]
    return jnp.concatenate(parts, axis=0)


def _level_decay(b, g, f, half, row):
    if half == 1:
        return jnp.where((row & 1) != 0, f, 1.0)
    if half == 2:
        up1 = pltpu.roll(g, BLOCK - 1, 0)
        dn1 = pltpu.roll(g, 1, 0)
        pos = row & 3
        d = jnp.where(pos == 0, up1, jnp.where(pos == 1, 0.0, jnp.where(pos == 2, g, g + dn1)))
        return jnp.exp(d)
    if half == 4:
        w = b.shape[-1]
        b3 = b.reshape(BLOCK // 8, 8, w)
        beta = jnp.broadcast_to(b3[:, 3:4, :], b3.shape).reshape(BLOCK, w)
        return jnp.exp(-jnp.abs(b - beta))
    parts = []
    for start in range(0, BLOCK, half):
        if start & half:
            ref = start - 1
            parts.append(b[start:start + half] - b[ref:ref + 1])
        else:
            ref = start + half - 1
            parts.append(b[ref:ref + 1] - b[start:start + half])
    return jnp.exp(jnp.concatenate(parts, axis=0))


def _pair_scores(a0, b0, a1, b1):
    z = jnp.zeros_like(b0)
    lhs = jnp.concatenate([a0, a1], axis=1)
    rhs = jnp.concatenate([jnp.concatenate([b0, z], axis=1), jnp.concatenate([z, b1], axis=1)], axis=0)
    out = _dot_nt(lhs, rhs)
    return out[:, :BLOCK], out[:, BLOCK:]


def _ada_kernel(c_ref, w_ref, b_ref, o_ref):
    cond = _silu(c_ref[...])
    c_hi, c_lo = _split_bf16(cond)
    w_hi, w_lo = _split_bf16(w_ref[...])
    o_ref[...] = _dot(c_hi, w_hi) + _dot(c_lo, w_hi) + _dot(c_hi, w_lo) + b_ref[...]


def _ada_call(c, w, b):
    bsz = c.shape[0]
    n = w.shape[1]
    tn = 512
    return pl.pallas_call(
        _ada_kernel,
        grid=(n // tn,),
        in_specs=[
            pl.BlockSpec((bsz, D_MODEL), lambda j: (0, 0)),
            pl.BlockSpec((D_MODEL, tn), lambda j: (0, j)),
            pl.BlockSpec((1, tn), lambda j: (0, j)),
        ],
        out_specs=pl.BlockSpec((bsz, tn), lambda j: (0, j)),
        out_shape=jax.ShapeDtypeStruct((bsz, n), F32),
        name="ada_mod",
    )(c, w, b)


PROJ_CHUNK = 256
MIX_SEQS = 4
KV_VARIANTS = 8


def _mixer_kernel(sink_ref, x_ref, xres_ref, mod_ref, g_mix_ref, w_in_ref, b_in_ref, attn_g_ref, lb_logit_ref,
                  hgrn_g_ref, w_out_ref, tri_ref, o_ref, kv_ref, state_ref, h_ref, proj_ref):
    j = pl.program_id(1)

    def reset_carry():
        kv_ref[:, 0:BLOCK, :] = jnp.zeros((MIX_SEQS * KV_VARIANTS, BLOCK, LANES_V7X), BF16)
        state_ref[...] = jnp.zeros_like(state_ref)

    @pl.when(j == 0)
    def _():
        reset_carry()
        h_ref[1] = jnp.zeros((MIX_SEQS * BLOCK, D_MODEL), BF16)
        proj_ref[0] = jnp.zeros((MIX_SEQS * BLOCK, IN_COLS), F32)

    @pl.when(j % 2 == 0)
    def _():
        _mixer_step(j, 0, sink_ref, x_ref, xres_ref, mod_ref, g_mix_ref, w_in_ref, b_in_ref, attn_g_ref,
                    lb_logit_ref, hgrn_g_ref, w_out_ref, tri_ref, o_ref, kv_ref, state_ref, h_ref, proj_ref)

    @pl.when(j % 2 == 1)
    def _():
        _mixer_step(j, 1, sink_ref, x_ref, xres_ref, mod_ref, g_mix_ref, w_in_ref, b_in_ref, attn_g_ref,
                    lb_logit_ref, hgrn_g_ref, w_out_ref, tri_ref, o_ref, kv_ref, state_ref, h_ref, proj_ref)

    @pl.when(j == 1)
    def _():
        reset_carry()


def _mixer_step(j, parity, sink_ref, x_ref, xres_ref, mod_ref, g_mix_ref, w_in_ref, b_in_ref, attn_g_ref,
                lb_logit_ref, hgrn_g_ref, w_out_ref, tri_ref, o_ref, kv_ref, state_ref, h_ref, proj_ref):
    seqs = range(MIX_SEQS)
    mix_slot = parity
    proj_slot = 1 - parity
    norm_slot = parity

    shift = [mod_ref[s, 0:1, :] for s in seqs]
    gate = [mod_ref[s, 2:3, :] for s in seqs]
    norm_gain = [g_mix_ref[...] * (1.0 + mod_ref[s, 1:2, :]) for s in seqs]

    l0 = lb_logit_ref[0:1, :]
    l1 = lb_logit_ref[1:2, :]
    lmax = jnp.maximum(l0, l1)
    e0 = jnp.exp(l0 - lmax)
    lb = e0 / (e0 + jnp.exp(l1 - lmax))
    tri = tri_ref[...]

    lane = lax.broadcasted_iota(jnp.int32, (BLOCK, LANES_V7X), 1)
    low = lane < ATTN_HEAD_DIM
    qi = lax.broadcasted_iota(jnp.int32, (BLOCK, 2 * BLOCK), 0)
    ki = lax.broadcasted_iota(jnp.int32, (BLOCK, 2 * BLOCK), 1)
    band = (ki > qi) & (ki <= qi + WINDOW)
    bias = jnp.where(band & ((j > 2) | (ki >= BLOCK)), 0.0, -jnp.inf)
    bias2 = jnp.concatenate([bias, bias], axis=0)

    ti = lax.broadcasted_iota(jnp.int32, (BLOCK, BLOCK), 0)
    si = lax.broadcasted_iota(jnp.int32, (BLOCK, BLOCK), 1)
    diff = ti ^ si
    lower = ti > si
    score_masks = [((diff >> (N_LEVELS - 1 - l)) == 1) & lower for l in range(N_LEVELS)] + [ti == si]
    roww = lax.broadcasted_iota(jnp.int32, (BLOCK, HGRN_WIDTH), 0)

    pending = list(range(IN_COLS // PROJ_CHUNK))

    def fill(count):
        for _ in range(min(count, len(pending))):
            c = pending.pop(0)
            cols = slice(c * PROJ_CHUNK, (c + 1) * PROJ_CHUNK)
            proj_ref[proj_slot, :, cols] = _dot(h_ref[proj_slot], w_in_ref[:, cols])

    def load(s, n):
        seg = slice(IN_OFFSETS[n], IN_OFFSETS[n + 1])
        return proj_ref[mix_slot, s * BLOCK:(s + 1) * BLOCK, seg] + b_in_ref[:, seg]

    fill(1)
    f, kf, g, b = [], [], [], []
    for s in seqs:
        f.append(lb + (1.0 - lb) * jax.nn.sigmoid(load(s, 4)))
        kf.append(1.0 - f[s])
        g.append(jnp.log(f[s]))
        g_hi, g_lo = _split_bf16(g[s])
        b.append(_dot(tri, g_hi) + _dot(tri, g_lo))

    logits = []
    for s in seqs:
        q_a, k_a, v_a = load(s, 0), load(s, 1), load(s, 2)
        zero = jnp.zeros((BLOCK, LANES_V7X), F32)
        k_roll = pltpu.roll(k_a, ATTN_HEAD_DIM, 1)
        v_roll = pltpu.roll(v_a, ATTN_HEAD_DIM, 1)
        variants = (
            (jnp.where(low, k_a, zero), jnp.where(low, v_a, zero)),
            (jnp.where(low, zero, k_roll), jnp.where(low, zero, v_roll)),
            (jnp.where(low, k_roll, zero), jnp.where(low, v_roll, zero)),
            (jnp.where(low, zero, k_a), jnp.where(low, zero, v_a)),
        )
        for n, (kk, vv) in enumerate(variants):
            kv_ref[KV_VARIANTS * s + n, BLOCK:2 * BLOCK, :] = kk.astype(BF16)
            kv_ref[KV_VARIANTS * s + 4 + n, BLOCK:2 * BLOCK, :] = vv.astype(BF16)
        q_scaled = (q_a * (ATTN_HEAD_DIM ** -0.5 * LOG2_E)).astype(BF16)
        per_kv = []
        for kvh in range(ATTN_KV_HEADS):
            qq = jnp.concatenate([q_scaled[:, (2 * kvh + r) * LANES_V7X:(2 * kvh + r + 1) * LANES_V7X]
                                  for r in range(2)], axis=0)
            per_kv.append([_dot_nt(qq, kv_ref[KV_VARIANTS * s + 2 * kvh + half]) + bias2
                           for half in range(2)])
        logits.append(per_kv)
        fill(1)
    q_inter, k_state, state_decay, lhs, rhs, gated, i_h = [], [], [], [], [], [], []
    for s in seqs:
        q_h = load(s, 3)
        i_h.append(load(s, 5))
        gated.append(_silu(load(s, 6)))
        b_last = b[s][BLOCK - 1:BLOCK, :]
        qf = _silu(q_h)
        q_inter.append((qf * jnp.exp(b[s])).astype(BF16))
        k_state.append((kf[s] * jnp.exp(b_last - b[s])).astype(BF16))
        state_decay.append(jnp.exp(b_last))
        operands = []
        for l in range(N_LEVELS):
            half = BLOCK >> (l + 1)
            if half >= 8:
                src = _rows_by_half(qf, kf[s], half)
            else:
                src = jnp.where((roww & half) != 0, qf, kf[s])
            operands.append((src * _level_decay(b[s], g[s], f[s], half, roww)).astype(BF16))
        lhs.append(operands + [qf.astype(BF16)])
        rhs.append(operands + [kf[s].astype(BF16)])
        fill(1)

    pairs = [[] for _ in seqs]
    scores = [[] for _ in seqs]
    for kvh in range(ATTN_KV_HEADS):
        for s in seqs:
            acc = [None, None]
            for half in range(2):
                sc = logits[s][kvh][half]
                probs, inv = [], []
                for r in range(2):
                    sink = sink_ref[2 * (2 * kvh + r) + half] * LOG2_E
                    rows = sc[r * BLOCK:(r + 1) * BLOCK]
                    m = jnp.maximum(jnp.max(rows, axis=-1, keepdims=True), sink)
                    e = jnp.exp2(rows - m)
                    inv.append(1.0 / (jnp.sum(e, axis=-1, keepdims=True) + jnp.exp2(sink - m)))
                    probs.append(e.astype(BF16))
                o = _dot(jnp.concatenate(probs, axis=0), kv_ref[KV_VARIANTS * s + 4 + 2 * kvh + half])
                for r in range(2):
                    term = o[r * BLOCK:(r + 1) * BLOCK] * inv[r]
                    acc[r] = term if acc[r] is None else acc[r] + term
            pairs[s].extend(acc)
            for hh in (2 * kvh, 2 * kvh + 1):
                cols = slice(hh * HGRN_DIM, (hh + 1) * HGRN_DIM)
                sc = jnp.zeros((BLOCK, BLOCK), F32)
                for n in range(0, N_LEVELS + 1, 2):
                    s0, s1 = _pair_scores(lhs[s][n][:, cols], rhs[s][n][:, cols],
                                          lhs[s][n + 1][:, cols], rhs[s][n + 1][:, cols])
                    sc = jnp.where(score_masks[n], s0, jnp.where(score_masks[n + 1], s1, sc))
                scores[s].append(sc.astype(BF16))
                fill(1)
    attn = []
    for s in seqs:
        attn_s = jnp.concatenate(pairs[s], axis=-1)
        attn.append((_rms_rows(attn_s) * attn_g_ref[...]).astype(BF16))
    mixed = _dot(jnp.concatenate(attn, axis=0), w_out_ref[0:ATTN_WIDTH, :])
    for s in seqs:
        rows = slice(KV_VARIANTS * s, KV_VARIANTS * (s + 1))
        kv_ref[rows, 0:BLOCK, :] = kv_ref[rows, BLOCK:2 * BLOCK, :]

    recs = [[] for _ in seqs]
    for hh in range(HGRN_HEADS):
        cols = slice(hh * HGRN_DIM, (hh + 1) * HGRN_DIM)
        for s in seqs:
            v_t = i_h[s][:, cols].T.astype(BF16)
            state = state_ref[HGRN_HEADS * s + hh]
            o = _dot_nt(jnp.concatenate([q_inter[s][:, cols], scores[s][hh]], axis=1),
                        jnp.concatenate([state.astype(BF16), v_t], axis=1))
            state_ref[HGRN_HEADS * s + hh] = state * state_decay[s][:, cols] + _dot(v_t, k_state[s][:, cols])
            recs[s].append((_rms_rows(o) * hgrn_g_ref[:, cols] * gated[s][:, cols]).astype(BF16))

    rec = jnp.concatenate([jnp.concatenate(recs[s], axis=-1) for s in seqs], axis=0)
    mixed = mixed + _dot(rec, w_out_ref[ATTN_WIDTH:, :])
    fill(len(pending))
    for s in seqs:
        o_ref[s] = xres_ref[s] + gate[s] * mixed[s * BLOCK:(s + 1) * BLOCK]

    for s in seqs:
        h_ref[norm_slot, s * BLOCK:(s + 1) * BLOCK, :] = (_rms_rows(x_ref[s]) * norm_gain[s] + shift[s]).astype(BF16)


def _mixer_call(x, mod, sinks, g_mix, w_in, b_in, attn_g, lb_logits, hgrn_g, w_out, tri):
    bsz, seq, _ = x.shape
    n_blocks = seq // BLOCK
    const = lambda b, j: (0, 0)
    return pl.pallas_call(
        _mixer_kernel,
        grid=(bsz // MIX_SEQS, n_blocks + 2),
        in_specs=[
            pl.BlockSpec(memory_space=pltpu.SMEM),
            pl.BlockSpec((MIX_SEQS, BLOCK, D_MODEL), lambda b, j: (b, jnp.minimum(j, n_blocks - 1), 0)),
            pl.BlockSpec((MIX_SEQS, BLOCK, D_MODEL), lambda b, j: (b, jnp.maximum(j - 2, 0), 0)),
            pl.BlockSpec((MIX_SEQS, N_MOD, D_MODEL), lambda b, j: (b, 0, 0)),
            pl.BlockSpec((1, D_MODEL), const),
            pl.BlockSpec((D_MODEL, IN_COLS), const),
            pl.BlockSpec((1, IN_COLS), const),
            pl.BlockSpec((1, ATTN_WIDTH), const),
            pl.BlockSpec((2, HGRN_WIDTH), const),
            pl.BlockSpec((1, HGRN_WIDTH), const),
            pl.BlockSpec((D_MODEL, D_MODEL), const),
            pl.BlockSpec((BLOCK, BLOCK), const),
        ],
        out_specs=pl.BlockSpec((MIX_SEQS, BLOCK, D_MODEL), lambda b, j: (b, jnp.maximum(j - 2, 0), 0)),
        out_shape=jax.ShapeDtypeStruct(x.shape, F32),
        scratch_shapes=[
            pltpu.VMEM((MIX_SEQS * KV_VARIANTS, 2 * BLOCK, LANES_V7X), BF16),
            pltpu.VMEM((MIX_SEQS * HGRN_HEADS, HGRN_DIM, HGRN_DIM), F32),
            pltpu.VMEM((2, MIX_SEQS * BLOCK, D_MODEL), BF16),
            pltpu.VMEM((2, MIX_SEQS * BLOCK, IN_COLS), F32),
        ],
        compiler_params=pltpu.CompilerParams(
            dimension_semantics=("arbitrary", "arbitrary"), vmem_limit_bytes=VMEM_LIMIT_BYTES),
        name="token_mixer",
    )(sinks, x, x, mod, g_mix, w_in, b_in, attn_g, lb_logits, hgrn_g, w_out, tri)


FFN_ROWS = 1024
FFN_CHUNK = 256
CONV_HALO = 8


def _ffn_kernel(x_ref, mod_ref, g_ffn_ref, w_up_ref, conv_w_ref, conv_b_ref, w_down_ref, g_fin_ref,
                o_ref, carry_ref, ubuf_ref, act_ref):
    j = pl.program_id(1)

    @pl.when(j == 0)
    def _():
        carry_ref[...] = jnp.zeros_like(carry_ref)

    x = x_ref[0]
    shift = mod_ref[0, 3:4, :]
    scale = mod_ref[0, 4:5, :]
    gate = mod_ref[0, 5:6, :]
    h = (_rms_rows(x) * g_ffn_ref[...] * (1.0 + scale) + shift).astype(BF16)

    for c in range(D_FF // FFN_CHUNK):
        cols = slice(c * FFN_CHUNK, (c + 1) * FFN_CHUNK)
        u = _dot(h, w_up_ref[:, cols])
        v = _dot(h, w_up_ref[:, D_FF + c * FFN_CHUNK:D_FF + (c + 1) * FFN_CHUNK])
        ubuf_ref[0:CONV_HALO, :] = carry_ref[:, cols]
        ubuf_ref[CONV_HALO:, :] = u
        carry_ref[:, cols] = u[FFN_ROWS - CONV_HALO:, :]
        u1 = ubuf_ref[CONV_HALO - 1:CONV_HALO - 1 + FFN_ROWS, :]
        u2 = ubuf_ref[CONV_HALO - 2:CONV_HALO - 2 + FFN_ROWS, :]
        conv = (conv_w_ref[0:1, cols] * u2 + conv_w_ref[1:2, cols] * u1 + conv_w_ref[2:3, cols] * u
                + conv_b_ref[:, cols])
        act_ref[:, cols] = (_silu(conv) * v).astype(BF16)

    y = x + gate * _dot(act_ref[...], w_down_ref[...])
    o_ref[0] = _rms_rows(y) * g_fin_ref[...]


def _ffn_call(x1, mod, g_ffn, w_up, conv_w, conv_b, w_down, g_fin):
    bsz, seq, _ = x1.shape
    const = lambda b, j: (0, 0)
    return pl.pallas_call(
        _ffn_kernel,
        grid=(bsz, seq // FFN_ROWS),
        in_specs=[
            pl.BlockSpec((1, FFN_ROWS, D_MODEL), lambda b, j: (b, j, 0)),
            pl.BlockSpec((1, N_MOD, D_MODEL), lambda b, j: (b, 0, 0)),
            pl.BlockSpec((1, D_MODEL), const),
            pl.BlockSpec((D_MODEL, 2 * D_FF), const, pipeline_mode=pl.Buffered(1)),
            pl.BlockSpec((3, D_FF), const),
            pl.BlockSpec((1, D_FF), const),
            pl.BlockSpec((D_FF, D_MODEL), const, pipeline_mode=pl.Buffered(1)),
            pl.BlockSpec((1, D_MODEL), const),
        ],
        out_specs=pl.BlockSpec((1, FFN_ROWS, D_MODEL), lambda b, j: (b, j, 0)),
        out_shape=jax.ShapeDtypeStruct(x1.shape, F32),
        scratch_shapes=[
            pltpu.VMEM((CONV_HALO, D_FF), F32),
            pltpu.VMEM((CONV_HALO + FFN_ROWS, FFN_CHUNK), F32),
            pltpu.VMEM((FFN_ROWS, D_FF), BF16),
        ],
        compiler_params=pltpu.CompilerParams(
            dimension_semantics=("arbitrary", "arbitrary"), vmem_limit_bytes=VMEM_LIMIT_BYTES),
        name="channel_mixer",
    )(x1, mod, g_ffn, w_up, conv_w, conv_b, w_down, g_fin)


def kernel(x, c, ada_w, ada_b, mix_norm_g, w_in, b_in, attn_sinks, attn_out_g, hgrn_lb_logits, hgrn_out_g,
           w_out, ffn_norm_g, w_up, conv_w, conv_b, w_down, final_norm_g):
    assert ada_w.shape[0] == 1 and x.shape[-1] == D_MODEL and x.shape[1] % FFN_ROWS == 0
    assert x.shape[0] % MIX_SEQS == 0
    bsz = x.shape[0]
    mod = _ada_call(c, ada_w[0], ada_b[0][None, :]).reshape(bsz, N_MOD, D_MODEL)
    tri = jnp.asarray(_cumsum_matrix(), dtype=BF16)
    x1 = _mixer_call(
        x, mod, attn_sinks[0], mix_norm_g[0][None, :], w_in[0].astype(BF16), b_in[0][None, :],
        attn_out_g[0][None, :], hgrn_lb_logits, hgrn_out_g[0][None, :], w_out[0].astype(BF16), tri)
    return _ffn_call(
        x1, mod, ffn_norm_g[0][None, :], w_up[0].astype(BF16), conv_w[0], conv_b[0][None, :],
        w_down[0].astype(BF16), final_norm_g[None, :])
```

```python
import functools

import numpy as np
import jax
import jax.numpy as jnp
from jax import lax
from jax.experimental import pallas as pl
from jax.experimental.pallas import tpu as pltpu

D_MODEL = 1024
ATTN_HEAD_DIM = 64
ATTN_WIDTH = 512
ATTN_HEADS = 8
ATTN_KV_HEADS = 2
ATTN_KV_WIDTH = 128
WINDOW = 128
HGRN_HEADS = 4
HGRN_DIM = 128
HGRN_WIDTH = 512
IN_SPLITS = (ATTN_WIDTH, ATTN_KV_WIDTH, ATTN_KV_WIDTH, HGRN_WIDTH, HGRN_WIDTH, HGRN_WIDTH, HGRN_WIDTH)
IN_OFFSETS = tuple(int(s) for s in np.cumsum((0,) + IN_SPLITS))
IN_COLS = IN_OFFSETS[-1]
D_FF = 2816
N_MOD = 6
EPS = 1e-6
LOG2_E = 1.4426950408889634

BLOCK = 128
N_LEVELS = 7
LANES_V7X = 128
VMEM_LIMIT_BYTES = 56 * 1024 * 1024

F32 = jnp.float32
BF16 = jnp.bfloat16


def _dot(a, b):
    return jnp.dot(a, b, preferred_element_type=F32)


def _dot_nt(a, b):
    return lax.dot_general(a, b, (((1,), (1,)), ((), ())), preferred_element_type=F32)


def _split_bf16(v):
    hi = v.astype(BF16)
    lo = (v - hi.astype(F32)).astype(BF16)
    return hi, lo


def _rms_rows(v):
    return v * lax.rsqrt(jnp.mean(v * v, axis=-1, keepdims=True) + EPS)


def _silu(v):
    return v * jax.nn.sigmoid(v)


def _cumsum_matrix():
    t = np.arange(BLOCK)[:, None]
    r = np.arange(BLOCK)[None, :]
    return (r <= t).astype(np.float32)


def _rows_by_half(upper_src, lower_src, half):
    parts = [(upper_src if (start & half) else lower_src)[start:start + half] for start in range(0, BLOCK, half)]
    return jnp.concatenate(parts, axis=0)


def _level_decay(b, g, f, half, row):
    if half == 1:
        return jnp.where((row & 1) != 0, f, 1.0)
    if half == 2:
        up1 = pltpu.roll(g, BLOCK - 1, 0)
        dn1 = pltpu.roll(g, 1, 0)
        pos = row & 3
        d = jnp.where(pos == 0, up1, jnp.where(pos == 1, 0.0, jnp.where(pos == 2, g, g + dn1)))
        return jnp.exp(d)
    if half == 4:
        w = b.shape[-1]
        b3 = b.reshape(BLOCK // 8, 8, w)
        beta = jnp.broadcast_to(b3[:, 3:4, :], b3.shape).reshape(BLOCK, w)
        return jnp.exp(-jnp.abs(b - beta))
    parts = []
    for start in range(0, BLOCK, half):
        if start & half:
            ref = start - 1
            parts.append(b[start:start + half] - b[ref:ref + 1])
        else:
            ref = start + half - 1
            parts.append(b[ref:ref + 1] - b[start:start + half])
    return jnp.exp(jnp.concatenate(parts, axis=0))


def _pair_scores(a0, b0, a1, b1):
    z = jnp.zeros_like(b0)
    lhs = jnp.concatenate([a0, a1], axis=1)
    rhs = jnp.concatenate([jnp.concatenate([b0, z], axis=1), jnp.concatenate([z, b1], axis=1)], axis=0)
    out = _dot_nt(lhs, rhs)
    return out[:, :BLOCK], out[:, BLOCK:]


def _ada_kernel(c_ref, w_ref, b_ref, o_ref):
    cond = _silu(c_ref[...])
    c_hi, c_lo = _split_bf16(cond)
    w_hi, w_lo = _split_bf16(w_ref[...])
    o_ref[...] = _dot(c_hi, w_hi) + _dot(c_lo, w_hi) + _dot(c_hi, w_lo) + b_ref[...]


def _ada_call(c, w, b):
    bsz = c.shape[0]
    n = w.shape[1]
    tn = 512
    return pl.pallas_call(
        _ada_kernel,
        grid=(n // tn,),
        in_specs=[
            pl.BlockSpec((bsz, D_MODEL), lambda j: (0, 0)),
            pl.BlockSpec((D_MODEL, tn), lambda j: (0, j)),
            pl.BlockSpec((1, tn), lambda j: (0, j)),
        ],
        out_specs=pl.BlockSpec((bsz, tn), lambda j: (0, j)),
        out_shape=jax.ShapeDtypeStruct((bsz, n), F32),
        name="ada_mod",
    )(c, w, b)


PROJ_CHUNK = 256
MIX_SEQS = 4
KV_VARIANTS = 8


def _mixer_kernel(sink_ref, x_ref, xres_ref, mod_norm_ref, mod_mix_ref, g_mix_ref, w_in_ref, b_in_ref, attn_g_ref,
                  lb_logit_ref, hgrn_g_ref, w_out_ref, tri_ref, o_ref, kv_ref, state_ref, h_ref, proj_ref, *,
                  n_blocks):
    t = pl.program_id(0)
    mix_block = (t - 2) % n_blocks

    @pl.when((t == 0) | (mix_block == 0))
    def _():
        kv_ref[:, 0:BLOCK, :] = jnp.zeros((MIX_SEQS * KV_VARIANTS, BLOCK, LANES_V7X), BF16)
        state_ref[...] = jnp.zeros_like(state_ref)

    @pl.when(t == 0)
    def _():
        h_ref[1] = jnp.zeros((MIX_SEQS * BLOCK, D_MODEL), BF16)
        proj_ref[0] = jnp.zeros((MIX_SEQS * BLOCK, IN_COLS), F32)

    @pl.when(t % 2 == 0)
    def _():
        _mixer_step(mix_block == 0, 0, sink_ref, x_ref, xres_ref, mod_norm_ref, mod_mix_ref, g_mix_ref, w_in_ref,
                    b_in_ref, attn_g_ref, lb_logit_ref, hgrn_g_ref, w_out_ref, tri_ref, o_ref, kv_ref, state_ref,
                    h_ref, proj_ref)

    @pl.when(t % 2 == 1)
    def _():
        _mixer_step(mix_block == 0, 1, sink_ref, x_ref, xres_ref, mod_norm_ref, mod_mix_ref, g_mix_ref, w_in_ref,
                    b_in_ref, attn_g_ref, lb_logit_ref, hgrn_g_ref, w_out_ref, tri_ref, o_ref, kv_ref, state_ref,
                    h_ref, proj_ref)


def _mixer_step(first_block, parity, sink_ref, x_ref, xres_ref, mod_norm_ref, mod_mix_ref, g_mix_ref, w_in_ref,
                b_in_ref, attn_g_ref, lb_logit_ref, hgrn_g_ref, w_out_ref, tri_ref, o_ref, kv_ref, state_ref,
                h_ref, proj_ref):
    seqs = range(MIX_SEQS)
    mix_slot = parity
    proj_slot = 1 - parity
    norm_slot = parity

    shift = [mod_norm_ref[s, 0:1, :] for s in seqs]
    norm_gain = [g_mix_ref[...] * (1.0 + mod_norm_ref[s, 1:2, :]) for s in seqs]
    gate = [mod_mix_ref[s, 2:3, :] for s in seqs]

    l0 = lb_logit_ref[0:1, :]
    l1 = lb_logit_ref[1:2, :]
    lmax = jnp.maximum(l0, l1)
    e0 = jnp.exp(l0 - lmax)
    lb = e0 / (e0 + jnp.exp(l1 - lmax))
    tri = tri_ref[...]

    lane = lax.broadcasted_iota(jnp.int32, (BLOCK, LANES_V7X), 1)
    low = lane < ATTN_HEAD_DIM
    qi = lax.broadcasted_iota(jnp.int32, (BLOCK, 2 * BLOCK), 0)
    ki = lax.broadcasted_iota(jnp.int32, (BLOCK, 2 * BLOCK), 1)
    band = (ki > qi) & (ki <= qi + WINDOW)
    bias = jnp.where(band & (jnp.logical_not(first_block) | (ki >= BLOCK)), 0.0, -jnp.inf)
    bias2 = jnp.concatenate([bias, bias], axis=0)

    ti = lax.broadcasted_iota(jnp.int32, (BLOCK, BLOCK), 0)
    si = lax.broadcasted_iota(jnp.int32, (BLOCK, BLOCK), 1)
    diff = ti ^ si
    lower = ti > si
    score_masks = [((diff >> (N_LEVELS - 1 - l)) == 1) & lower for l in range(N_LEVELS)] + [ti == si]
    roww = lax.broadcasted_iota(jnp.int32, (BLOCK, HGRN_WIDTH), 0)

    pending = list(range(IN_COLS // PROJ_CHUNK))

    def fill(count):
        for _ in range(min(count, len(pending))):
            c = pending.pop(0)
            cols = slice(c * PROJ_CHUNK, (c + 1) * PROJ_CHUNK)
            proj_ref[proj_slot, :, cols] = _dot(h_ref[proj_slot], w_in_ref[:, cols])

    def load(s, n):
        seg = slice(IN_OFFSETS[n], IN_OFFSETS[n + 1])
        return proj_ref[mix_slot, s * BLOCK:(s + 1) * BLOCK, seg] + b_in_ref[:, seg]

    fill(1)
    f, kf, g, b = [], [], [], []
    for s in seqs:
        f.append(lb + (1.0 - lb) * jax.nn.sigmoid(load(s, 4)))
        kf.append(1.0 - f[s])
        g.append(jnp.log(f[s]))
        g_hi, g_lo = _split_bf16(g[s])
        b.append(_dot(tri, g_hi) + _dot(tri, g_lo))

    logits = []
    for s in seqs:
        q_a, k_a, v_a = load(s, 0), load(s, 1), load(s, 2)
        zero = jnp.zeros((BLOCK, LANES_V7X), F32)
        k_roll = pltpu.roll(k_a, ATTN_HEAD_DIM, 1)
        v_roll = pltpu.roll(v_a, ATTN_HEAD_DIM, 1)
        variants = (
            (jnp.where(low, k_a, zero), jnp.where(low, v_a, zero)),
            (jnp.where(low, zero, k_roll), jnp.where(low, zero, v_roll)),
            (jnp.where(low, k_roll, zero), jnp.where(low, v_roll, zero)),
            (jnp.where(low, zero, k_a), jnp.where(low, zero, v_a)),
        )
        for n, (kk, vv) in enumerate(variants):
            kv_ref[KV_VARIANTS * s + n, BLOCK:2 * BLOCK, :] = kk.astype(BF16)
            kv_ref[KV_VARIANTS * s + 4 + n, BLOCK:2 * BLOCK, :] = vv.astype(BF16)
        q_scaled = (q_a * (ATTN_HEAD_DIM ** -0.5 * LOG2_E)).astype(BF16)
        per_kv = []
        for kvh in range(ATTN_KV_HEADS):
            qq = jnp.concatenate([q_scaled[:, (2 * kvh + r) * LANES_V7X:(2 * kvh + r + 1) * LANES_V7X]
                                  for r in range(2)], axis=0)
            per_kv.append([_dot_nt(qq, kv_ref[KV_VARIANTS * s + 2 * kvh + half]) + bias2
                           for half in range(2)])
        logits.append(per_kv)
        fill(1)
    q_inter, k_state, state_decay, lhs, rhs, gated, i_h = [], [], [], [], [], [], []
    for s in seqs:
        q_h = load(s, 3)
        i_h.append(load(s, 5))
        gated.append(_silu(load(s, 6)))
        b_last = b[s][BLOCK - 1:BLOCK, :]
        qf = _silu(q_h)
        q_inter.append((qf * jnp.exp(b[s])).astype(BF16))
        k_state.append((kf[s] * jnp.exp(b_last - b[s])).astype(BF16))
        state_decay.append(jnp.exp(b_last))
        operands = []
        for l in range(N_LEVELS):
            half = BLOCK >> (l + 1)
            if half >= 8:
                src = _rows_by_half(qf, kf[s], half)
            else:
                src = jnp.where((roww & half) != 0, qf, kf[s])
            operands.append((src * _level_decay(b[s], g[s], f[s], half, roww)).astype(BF16))
        lhs.append(operands + [qf.astype(BF16)])
        rhs.append(operands + [kf[s].astype(BF16)])
        fill(1)

    pairs = [[] for _ in seqs]
    scores = [[] for _ in seqs]
    for kvh in range(ATTN_KV_HEADS):
        for s in seqs:
            acc = [None, None]
            for half in range(2):
                sc = logits[s][kvh][half]
                probs, inv = [], []
                for r in range(2):
                    sink = sink_ref[2 * (2 * kvh + r) + half] * LOG2_E
                    rows = sc[r * BLOCK:(r + 1) * BLOCK]
                    m = jnp.maximum(jnp.max(rows, axis=-1, keepdims=True), sink)
                    e = jnp.exp2(rows - m)
                    inv.append(1.0 / (jnp.sum(e, axis=-1, keepdims=True) + jnp.exp2(sink - m)))
                    probs.append(e.astype(BF16))
                o = _dot(jnp.concatenate(probs, axis=0), kv_ref[KV_VARIANTS * s + 4 + 2 * kvh + half])
                for r in range(2):
                    term = o[r * BLOCK:(r + 1) * BLOCK] * inv[r]
                    acc[r] = term if acc[r] is None else acc[r] + term
            pairs[s].extend(acc)
            for hh in (2 * kvh, 2 * kvh + 1):
                cols = slice(hh * HGRN_DIM, (hh + 1) * HGRN_DIM)
                sc = jnp.zeros((BLOCK, BLOCK), F32)
                for n in range(0, N_LEVELS + 1, 2):
                    s0, s1 = _pair_scores(lhs[s][n][:, cols], rhs[s][n][:, cols],
                                          lhs[s][n + 1][:, cols], rhs[s][n + 1][:, cols])
                    sc = jnp.where(score_masks[n], s0, jnp.where(score_masks[n + 1], s1, sc))
                scores[s].append(sc.astype(BF16))
                fill(1)
    attn = []
    for s in seqs:
        attn_s = jnp.concatenate(pairs[s], axis=-1)
        attn.append((_rms_rows(attn_s) * attn_g_ref[...]).astype(BF16))
    mixed = _dot(jnp.concatenate(attn, axis=0), w_out_ref[0:ATTN_WIDTH, :])
    for s in seqs:
        rows = slice(KV_VARIANTS * s, KV_VARIANTS * (s + 1))
        kv_ref[rows, 0:BLOCK, :] = kv_ref[rows, BLOCK:2 * BLOCK, :]

    recs = [[] for _ in seqs]
    for hh in range(HGRN_HEADS):
        cols = slice(hh * HGRN_DIM, (hh + 1) * HGRN_DIM)
        for s in seqs:
            v_t = i_h[s][:, cols].T.astype(BF16)
            state = state_ref[HGRN_HEADS * s + hh]
            o = _dot_nt(jnp.concatenate([q_inter[s][:, cols], scores[s][hh]], axis=1),
                        jnp.concatenate([state.astype(BF16), v_t], axis=1))
            state_ref[HGRN_HEADS * s + hh] = state * state_decay[s][:, cols] + _dot(v_t, k_state[s][:, cols])
            recs[s].append((_rms_rows(o) * hgrn_g_ref[:, cols] * gated[s][:, cols]).astype(BF16))

    rec = jnp.concatenate([jnp.concatenate(recs[s], axis=-1) for s in seqs], axis=0)
    mixed = mixed + _dot(rec, w_out_ref[ATTN_WIDTH:, :])
    fill(len(pending))
    for s in seqs:
        o_ref[s] = xres_ref[s] + gate[s] * mixed[s * BLOCK:(s + 1) * BLOCK]

    for s in seqs:
        h_ref[norm_slot, s * BLOCK:(s + 1) * BLOCK, :] = (_rms_rows(x_ref[s]) * norm_gain[s] + shift[s]).astype(BF16)


def _mixer_call(x, mod, sinks, g_mix, w_in, b_in, attn_g, lb_logits, hgrn_g, w_out, tri):
    bsz, seq, _ = x.shape
    n_blocks = seq // BLOCK
    n_steps = (bsz // MIX_SEQS) * n_blocks
    const = lambda t: (0, 0)
    norm_at = lambda t: jnp.minimum(t, n_steps - 1)
    mix_at = lambda t: jnp.maximum(t - 2, 0)
    return pl.pallas_call(
        functools.partial(_mixer_kernel, n_blocks=n_blocks),
        grid=(n_steps + 2,),
        in_specs=[
            pl.BlockSpec(memory_space=pltpu.SMEM),
            pl.BlockSpec((MIX_SEQS, BLOCK, D_MODEL), lambda t: (norm_at(t) // n_blocks, norm_at(t) % n_blocks, 0)),
            pl.BlockSpec((MIX_SEQS, BLOCK, D_MODEL), lambda t: (mix_at(t) // n_blocks, mix_at(t) % n_blocks, 0)),
            pl.BlockSpec((MIX_SEQS, N_MOD, D_MODEL), lambda t: (norm_at(t) // n_blocks, 0, 0)),
            pl.BlockSpec((MIX_SEQS, N_MOD, D_MODEL), lambda t: (mix_at(t) // n_blocks, 0, 0)),
            pl.BlockSpec((1, D_MODEL), const),
            pl.BlockSpec((D_MODEL, IN_COLS), const),
            pl.BlockSpec((1, IN_COLS), const),
            pl.BlockSpec((1, ATTN_WIDTH), const),
            pl.BlockSpec((2, HGRN_WIDTH), const),
            pl.BlockSpec((1, HGRN_WIDTH), const),
            pl.BlockSpec((D_MODEL, D_MODEL), const),
            pl.BlockSpec((BLOCK, BLOCK), const),
        ],
        out_specs=pl.BlockSpec((MIX_SEQS, BLOCK, D_MODEL),
                               lambda t: (mix_at(t) // n_blocks, mix_at(t) % n_blocks, 0)),
        out_shape=jax.ShapeDtypeStruct(x.shape, F32),
        scratch_shapes=[
            pltpu.VMEM((MIX_SEQS * KV_VARIANTS, 2 * BLOCK, LANES_V7X), BF16),
            pltpu.VMEM((MIX_SEQS * HGRN_HEADS, HGRN_DIM, HGRN_DIM), F32),
            pltpu.VMEM((2, MIX_SEQS * BLOCK, D_MODEL), BF16),
            pltpu.VMEM((2, MIX_SEQS * BLOCK, IN_COLS), F32),
        ],
        compiler_params=pltpu.CompilerParams(
            dimension_semantics=("arbitrary",), vmem_limit_bytes=VMEM_LIMIT_BYTES),
        name="token_mixer",
    )(sinks, x, x, mod, mod, g_mix, w_in, b_in, attn_g, lb_logits, hgrn_g, w_out, tri)


FFN_ROWS = 1024
FFN_CHUNK = 256
CONV_HALO = 8


def _ffn_kernel(x_ref, mod_ref, g_ffn_ref, w_up_ref, conv_w_ref, conv_b_ref, w_down_ref, g_fin_ref,
                o_ref, carry_ref, ubuf_ref, act_ref):
    j = pl.program_id(1)

    @pl.when(j == 0)
    def _():
        carry_ref[...] = jnp.zeros_like(carry_ref)

    x = x_ref[0]
    shift = mod_ref[0, 3:4, :]
    scale = mod_ref[0, 4:5, :]
    gate = mod_ref[0, 5:6, :]
    h = (_rms_rows(x) * g_ffn_ref[...] * (1.0 + scale) + shift).astype(BF16)

    for c in range(D_FF // FFN_CHUNK):
        cols = slice(c * FFN_CHUNK, (c + 1) * FFN_CHUNK)
        u = _dot(h, w_up_ref[:, cols])
        v = _dot(h, w_up_ref[:, D_FF + c * FFN_CHUNK:D_FF + (c + 1) * FFN_CHUNK])
        ubuf_ref[0:CONV_HALO, :] = carry_ref[:, cols]
        ubuf_ref[CONV_HALO:, :] = u
        carry_ref[:, cols] = u[FFN_ROWS - CONV_HALO:, :]
        u1 = ubuf_ref[CONV_HALO - 1:CONV_HALO - 1 + FFN_ROWS, :]
        u2 = ubuf_ref[CONV_HALO - 2:CONV_HALO - 2 + FFN_ROWS, :]
        conv = (conv_w_ref[0:1, cols] * u2 + conv_w_ref[1:2, cols] * u1 + conv_w_ref[2:3, cols] * u
                + conv_b_ref[:, cols])
        act_ref[:, cols] = (_silu(conv) * v).astype(BF16)

    y = x + gate * _dot(act_ref[...], w_down_ref[...])
    o_ref[0] = _rms_rows(y) * g_fin_ref[...]


def _ffn_call(x1, mod, g_ffn, w_up, conv_w, conv_b, w_down, g_fin):
    bsz, seq, _ = x1.shape
    const = lambda b, j: (0, 0)
    return pl.pallas_call(
        _ffn_kernel,
        grid=(bsz, seq // FFN_ROWS),
        in_specs=[
            pl.BlockSpec((1, FFN_ROWS, D_MODEL), lambda b, j: (b, j, 0)),
            pl.BlockSpec((1, N_MOD, D_MODEL), lambda b, j: (b, 0, 0)),
            pl.BlockSpec((1, D_MODEL), const),
            pl.BlockSpec((D_MODEL, 2 * D_FF), const, pipeline_mode=pl.Buffered(1)),
            pl.BlockSpec((3, D_FF), const),
            pl.BlockSpec((1, D_FF), const),
            pl.BlockSpec((D_FF, D_MODEL), const, pipeline_mode=pl.Buffered(1)),
            pl.BlockSpec((1, D_MODEL), const),
        ],
        out_specs=pl.BlockSpec((1, FFN_ROWS, D_MODEL), lambda b, j: (b, j, 0)),
        out_shape=jax.ShapeDtypeStruct(x1.shape, F32),
        scratch_shapes=[
            pltpu.VMEM((CONV_HALO, D_FF), F32),
            pltpu.VMEM((CONV_HALO + FFN_ROWS, FFN_CHUNK), F32),
            pltpu.VMEM((FFN_ROWS, D_FF), BF16),
        ],
        compiler_params=pltpu.CompilerParams(
            dimension_semantics=("arbitrary", "arbitrary"), vmem_limit_bytes=VMEM_LIMIT_BYTES),
        name="channel_mixer",
    )(x1, mod, g_ffn, w_up, conv_w, conv_b, w_down, g_fin)


def kernel(x, c, ada_w, ada_b, mix_norm_g, w_in, b_in, attn_sinks, attn_out_g, hgrn_lb_logits, hgrn_out_g,
           w_out, ffn_norm_g, w_up, conv_w, conv_b, w_down, final_norm_g):
    assert ada_w.shape[0] == 1 and x.shape[-1] == D_MODEL and x.shape[1] % FFN_ROWS == 0
    assert x.shape[0] % MIX_SEQS == 0
    bsz = x.shape[0]
    mod = _ada_call(c, ada_w[0], ada_b[0][None, :]).reshape(bsz, N_MOD, D_MODEL)
    tri = jnp.asarray(_cumsum_matrix(), dtype=BF16)
    x1 = _mixer_call(
        x, mod, attn_sinks[0], mix_norm_g[0][None, :], w_in[0].astype(BF16), b_in[0][None, :],
        attn_out_g[0][None, :], hgrn_lb_logits, hgrn_out_g[0][None, :], w_out[0].astype(BF16), tri)
    return _ffn_call(
        x1, mod, ffn_norm_g[0][None, :], w_up[0].astype(BF16), conv_w[0], conv_b[0][None, :],
        w_down[0].astype(BF16), final_norm_g[None, :])
```

```python
import functools

import numpy as np
import jax
import jax.numpy as jnp
from jax import lax
from jax.experimental import pallas as pl
from jax.experimental.pallas import tpu as pltpu

D_MODEL = 1024
ATTN_HEAD_DIM = 64
ATTN_WIDTH = 512
ATTN_HEADS = 8
ATTN_KV_HEADS = 2
ATTN_KV_WIDTH = 128
WINDOW = 128
HGRN_HEADS = 4
HGRN_DIM = 128
HGRN_WIDTH = 512
IN_SPLITS = (ATTN_WIDTH, ATTN_KV_WIDTH, ATTN_KV_WIDTH, HGRN_WIDTH, HGRN_WIDTH, HGRN_WIDTH, HGRN_WIDTH)
IN_OFFSETS = tuple(int(s) for s in np.cumsum((0,) + IN_SPLITS))
IN_COLS = IN_OFFSETS[-1]
D_FF = 2816
N_MOD = 6
EPS = 1e-6
LOG2_E = 1.4426950408889634

BLOCK = 128
LOG2_BLOCK = 7
N_LEVELS = 6
SUBLANES_V7X = 8
LANES_V7X = 128
VMEM_LIMIT_BYTES = 56 * 1024 * 1024

F32 = jnp.float32
BF16 = jnp.bfloat16


def _dot(a, b):
    return jnp.dot(a, b, preferred_element_type=F32)


def _dot_nt(a, b):
    return lax.dot_general(a, b, (((1,), (1,)), ((), ())), preferred_element_type=F32)


def _split_bf16(v):
    hi = v.astype(BF16)
    lo = (v - hi.astype(F32)).astype(BF16)
    return hi, lo


def _rms_rows(v):
    return v * lax.rsqrt(jnp.mean(v * v, axis=-1, keepdims=True) + EPS)


def _silu(v):
    return v * jax.nn.sigmoid(v)


def _cumsum_matrix():
    t = np.arange(BLOCK)[:, None]
    r = np.arange(BLOCK)[None, :]
    return (r <= t).astype(np.float32)


def _rows_by_half(upper_src, lower_src, half):
    parts = [(upper_src if (start & half) else lower_src)[start:start + half] for start in range(0, BLOCK, half)]
    return jnp.concatenate(parts, axis=0)


def _level_decay(b, g, half, row):
    if half == 2:
        up1 = pltpu.roll(g, BLOCK - 1, 0)
        dn1 = pltpu.roll(g, 1, 0)
        pos = row & 3
        d = jnp.where(pos == 0, up1, jnp.where(pos == 1, 0.0, jnp.where(pos == 2, g, g + dn1)))
        return jnp.exp(d)
    if half == SUBLANES_V7X // 2:
        w = b.shape[-1]
        b3 = b.reshape(BLOCK // SUBLANES_V7X, SUBLANES_V7X, w)
        beta = jnp.broadcast_to(b3[:, half - 1:half, :], b3.shape).reshape(BLOCK, w)
        return jnp.exp(-jnp.abs(b - beta))
    parts = []
    for start in range(0, BLOCK, half):
        if start & half:
            ref = start - 1
            parts.append(b[start:start + half] - b[ref:ref + 1])
        else:
            ref = start + half - 1
            parts.append(b[ref:ref + 1] - b[start:start + half])
    return jnp.exp(jnp.concatenate(parts, axis=0))


def _pair_scores(a0, b0, a1, b1):
    z = jnp.zeros_like(b0)
    lhs = jnp.concatenate([a0, a1], axis=1)
    rhs = jnp.concatenate([jnp.concatenate([b0, z], axis=1), jnp.concatenate([z, b1], axis=1)], axis=0)
    out = _dot_nt(lhs, rhs)
    return out[:, :BLOCK], out[:, BLOCK:]


def _ada_kernel(c_ref, w_ref, b_ref, o_ref):
    cond = _silu(c_ref[...])
    c_hi, c_lo = _split_bf16(cond)
    w_hi, w_lo = _split_bf16(w_ref[...])
    o_ref[...] = _dot(c_hi, w_hi) + _dot(c_lo, w_hi) + _dot(c_hi, w_lo) + b_ref[...]


def _ada_call(c, w, b):
    bsz = c.shape[0]
    n = w.shape[1]
    tn = 512
    return pl.pallas_call(
        _ada_kernel,
        grid=(n // tn,),
        in_specs=[
            pl.BlockSpec((bsz, D_MODEL), lambda j: (0, 0)),
            pl.BlockSpec((D_MODEL, tn), lambda j: (0, j)),
            pl.BlockSpec((1, tn), lambda j: (0, j)),
        ],
        out_specs=pl.BlockSpec((bsz, tn), lambda j: (0, j)),
        out_shape=jax.ShapeDtypeStruct((bsz, n), F32),
        name="ada_mod",
    )(c, w, b)


PROJ_CHUNK = 256
MIX_SEQS = 4
KV_VARIANTS = 8


def _mixer_kernel(sink_ref, x_ref, xres_ref, mod_norm_ref, mod_mix_ref, g_mix_ref, w_in_ref, b_in_ref, attn_g_ref,
                  lb_logit_ref, hgrn_g_ref, w_out_ref, tri_ref, o_ref, kv_ref, state_ref, h_ref, proj_ref, *,
                  n_blocks):
    t = pl.program_id(0)
    mix_block = (t - 2) % n_blocks

    @pl.when((t == 0) | (mix_block == 0))
    def _():
        kv_ref[:, 0:BLOCK, :] = jnp.zeros((MIX_SEQS * KV_VARIANTS, BLOCK, LANES_V7X), BF16)
        state_ref[...] = jnp.zeros_like(state_ref)

    @pl.when(t == 0)
    def _():
        h_ref[1] = jnp.zeros((MIX_SEQS * BLOCK, D_MODEL), BF16)
        proj_ref[0] = jnp.zeros((MIX_SEQS * BLOCK, IN_COLS), F32)

    @pl.when(t % 2 == 0)
    def _():
        _mixer_step(mix_block == 0, 0, sink_ref, x_ref, xres_ref, mod_norm_ref, mod_mix_ref, g_mix_ref, w_in_ref,
                    b_in_ref, attn_g_ref, lb_logit_ref, hgrn_g_ref, w_out_ref, tri_ref, o_ref, kv_ref, state_ref,
                    h_ref, proj_ref)

    @pl.when(t % 2 == 1)
    def _():
        _mixer_step(mix_block == 0, 1, sink_ref, x_ref, xres_ref, mod_norm_ref, mod_mix_ref, g_mix_ref, w_in_ref,
                    b_in_ref, attn_g_ref, lb_logit_ref, hgrn_g_ref, w_out_ref, tri_ref, o_ref, kv_ref, state_ref,
                    h_ref, proj_ref)


def _mixer_step(first_block, parity, sink_ref, x_ref, xres_ref, mod_norm_ref, mod_mix_ref, g_mix_ref, w_in_ref,
                b_in_ref, attn_g_ref, lb_logit_ref, hgrn_g_ref, w_out_ref, tri_ref, o_ref, kv_ref, state_ref,
                h_ref, proj_ref):
    seqs = range(MIX_SEQS)
    mix_slot = parity
    proj_slot = 1 - parity
    norm_slot = parity

    shift = [mod_norm_ref[s, 0:1, :] for s in seqs]
    norm_gain = [g_mix_ref[...] * (1.0 + mod_norm_ref[s, 1:2, :]) for s in seqs]
    gate = [mod_mix_ref[s, 2:3, :] for s in seqs]

    l0 = lb_logit_ref[0:1, :]
    l1 = lb_logit_ref[1:2, :]
    lmax = jnp.maximum(l0, l1)
    e0 = jnp.exp(l0 - lmax)
    lb = e0 / (e0 + jnp.exp(l1 - lmax))
    tri = tri_ref[...]

    lane = lax.broadcasted_iota(jnp.int32, (BLOCK, LANES_V7X), 1)
    low = lane < ATTN_HEAD_DIM
    qi = lax.broadcasted_iota(jnp.int32, (BLOCK, 2 * BLOCK), 0)
    ki = lax.broadcasted_iota(jnp.int32, (BLOCK, 2 * BLOCK), 1)
    band = (ki > qi) & (ki <= qi + WINDOW)
    bias = jnp.where(band & (jnp.logical_not(first_block) | (ki >= BLOCK)), 0.0, -jnp.inf)
    bias2 = jnp.concatenate([bias, bias], axis=0)

    ti = lax.broadcasted_iota(jnp.int32, (BLOCK, BLOCK), 0)
    si = lax.broadcasted_iota(jnp.int32, (BLOCK, BLOCK), 1)
    diff = ti ^ si
    lower = ti > si
    score_masks = [((diff >> (LOG2_BLOCK - 1 - l)) == 1) & lower for l in range(N_LEVELS)]
    diag_mask = ti == si
    adjacent_mask = (ti - si == 1) & ((ti & 1) == 1)
    roww = lax.broadcasted_iota(jnp.int32, (BLOCK, HGRN_WIDTH), 0)

    pending = list(range(IN_COLS // PROJ_CHUNK))

    def fill(count):
        for _ in range(min(count, len(pending))):
            c = pending.pop(0)
            cols = slice(c * PROJ_CHUNK, (c + 1) * PROJ_CHUNK)
            proj_ref[proj_slot, :, cols] = _dot(h_ref[proj_slot], w_in_ref[:, cols])

    def load(s, n):
        seg = slice(IN_OFFSETS[n], IN_OFFSETS[n + 1])
        return proj_ref[mix_slot, s * BLOCK:(s + 1) * BLOCK, seg] + b_in_ref[:, seg]

    fill(1)
    f, kf, g, b = [], [], [], []
    for s in seqs:
        f.append(lb + (1.0 - lb) * jax.nn.sigmoid(load(s, 4)))
        kf.append(1.0 - f[s])
        g.append(jnp.log(f[s]))
        g_hi, g_lo = _split_bf16(g[s])
        b.append(_dot(tri, g_hi) + _dot(tri, g_lo))

    logits = []
    for s in seqs:
        q_a, k_a, v_a = load(s, 0), load(s, 1), load(s, 2)
        zero = jnp.zeros((BLOCK, LANES_V7X), F32)
        k_roll = pltpu.roll(k_a, ATTN_HEAD_DIM, 1)
        v_roll = pltpu.roll(v_a, ATTN_HEAD_DIM, 1)
        variants = (
            (jnp.where(low, k_a, zero), jnp.where(low, v_a, zero)),
            (jnp.where(low, zero, k_roll), jnp.where(low, zero, v_roll)),
            (jnp.where(low, k_roll, zero), jnp.where(low, v_roll, zero)),
            (jnp.where(low, zero, k_a), jnp.where(low, zero, v_a)),
        )
        for n, (kk, vv) in enumerate(variants):
            kv_ref[KV_VARIANTS * s + n, BLOCK:2 * BLOCK, :] = kk.astype(BF16)
            kv_ref[KV_VARIANTS * s + 4 + n, BLOCK:2 * BLOCK, :] = vv.astype(BF16)
        q_scaled = (q_a * (ATTN_HEAD_DIM ** -0.5 * LOG2_E)).astype(BF16)
        per_kv = []
        for kvh in range(ATTN_KV_HEADS):
            qq = jnp.concatenate([q_scaled[:, (2 * kvh + r) * LANES_V7X:(2 * kvh + r + 1) * LANES_V7X]
                                  for r in range(2)], axis=0)
            per_kv.append([_dot_nt(qq, kv_ref[KV_VARIANTS * s + 2 * kvh + half]) + bias2
                           for half in range(2)])
        logits.append(per_kv)
        fill(1)
    q_inter, k_state, state_decay, lhs, rhs, gated, i_h, near = [], [], [], [], [], [], [], []
    for s in seqs:
        q_h = load(s, 3)
        i_h.append(load(s, 5))
        gated.append(_silu(load(s, 6)))
        b_last = b[s][BLOCK - 1:BLOCK, :]
        qf = _silu(q_h)
        q_inter.append((qf * jnp.exp(b[s])).astype(BF16))
        k_state.append((kf[s] * jnp.exp(b_last - b[s])).astype(BF16))
        state_decay.append(jnp.exp(b_last))
        operands = []
        for l in range(N_LEVELS):
            half = BLOCK >> (l + 1)
            if half >= SUBLANES_V7X:
                src = _rows_by_half(qf, kf[s], half)
            else:
                src = jnp.where((roww & half) != 0, qf, kf[s])
            operands.append((src * _level_decay(b[s], g[s], half, roww)).astype(BF16))
        lhs.append(operands)
        rhs.append(operands)
        near.append((qf * kf[s], qf * f[s] * pltpu.roll(kf[s], 1, 0)))
        fill(1)

    pairs = [[] for _ in seqs]
    scores = [[] for _ in seqs]
    for kvh in range(ATTN_KV_HEADS):
        for s in seqs:
            acc = [None, None]
            for half in range(2):
                sc = logits[s][kvh][half]
                probs, inv = [], []
                for r in range(2):
                    sink = sink_ref[2 * (2 * kvh + r) + half] * LOG2_E
                    rows = sc[r * BLOCK:(r + 1) * BLOCK]
                    m = jnp.maximum(jnp.max(rows, axis=-1, keepdims=True), sink)
                    e = jnp.exp2(rows - m)
                    inv.append(1.0 / (jnp.sum(e, axis=-1, keepdims=True) + jnp.exp2(sink - m)))
                    probs.append(e.astype(BF16))
                o = _dot(jnp.concatenate(probs, axis=0), kv_ref[KV_VARIANTS * s + 4 + 2 * kvh + half])
                for r in range(2):
                    term = o[r * BLOCK:(r + 1) * BLOCK] * inv[r]
                    acc[r] = term if acc[r] is None else acc[r] + term
            pairs[s].extend(acc)
            for hh in (2 * kvh, 2 * kvh + 1):
                cols = slice(hh * HGRN_DIM, (hh + 1) * HGRN_DIM)
                sc = jnp.where(diag_mask, jnp.sum(near[s][0][:, cols], axis=-1, keepdims=True),
                               jnp.where(adjacent_mask, jnp.sum(near[s][1][:, cols], axis=-1, keepdims=True), 0.0))
                for n in range(0, N_LEVELS, 2):
                    s0, s1 = _pair_scores(lhs[s][n][:, cols], rhs[s][n][:, cols],
                                          lhs[s][n + 1][:, cols], rhs[s][n + 1][:, cols])
                    sc = jnp.where(score_masks[n], s0, jnp.where(score_masks[n + 1], s1, sc))
                scores[s].append(sc.astype(BF16))
                fill(1)
    attn = []
    for s in seqs:
        attn_s = jnp.concatenate(pairs[s], axis=-1)
        attn.append((_rms_rows(attn_s) * attn_g_ref[...]).astype(BF16))
    mixed = _dot(jnp.concatenate(attn, axis=0), w_out_ref[0:ATTN_WIDTH, :])
    for s in seqs:
        rows = slice(KV_VARIANTS * s, KV_VARIANTS * (s + 1))
        kv_ref[rows, 0:BLOCK, :] = kv_ref[rows, BLOCK:2 * BLOCK, :]

    recs = [[] for _ in seqs]
    for hh in range(HGRN_HEADS):
        cols = slice(hh * HGRN_DIM, (hh + 1) * HGRN_DIM)
        for s in seqs:
            v_t = i_h[s][:, cols].T.astype(BF16)
            state = state_ref[HGRN_HEADS * s + hh]
            o = _dot_nt(jnp.concatenate([q_inter[s][:, cols], scores[s][hh]], axis=1),
                        jnp.concatenate([state.astype(BF16), v_t], axis=1))
            state_ref[HGRN_HEADS * s + hh] = state * state_decay[s][:, cols] + _dot(v_t, k_state[s][:, cols])
            recs[s].append((_rms_rows(o) * hgrn_g_ref[:, cols] * gated[s][:, cols]).astype(BF16))

    rec = jnp.concatenate([jnp.concatenate(recs[s], axis=-1) for s in seqs], axis=0)
    mixed = mixed + _dot(rec, w_out_ref[ATTN_WIDTH:, :])
    fill(len(pending))
    for s in seqs:
        o_ref[s] = xres_ref[s] + gate[s] * mixed[s * BLOCK:(s + 1) * BLOCK]

    for s in seqs:
        h_ref[norm_slot, s * BLOCK:(s + 1) * BLOCK, :] = (_rms_rows(x_ref[s]) * norm_gain[s] + shift[s]).astype(BF16)


def _mixer_call(x, mod, sinks, g_mix, w_in, b_in, attn_g, lb_logits, hgrn_g, w_out, tri):
    bsz, seq, _ = x.shape
    n_blocks = seq // BLOCK
    n_steps = (bsz // MIX_SEQS) * n_blocks
    const = lambda t: (0, 0)
    norm_at = lambda t: jnp.minimum(t, n_steps - 1)
    mix_at = lambda t: jnp.maximum(t - 2, 0)
    return pl.pallas_call(
        functools.partial(_mixer_kernel, n_blocks=n_blocks),
        grid=(n_steps + 2,),
        in_specs=[
            pl.BlockSpec(memory_space=pltpu.SMEM),
            pl.BlockSpec((MIX_SEQS, BLOCK, D_MODEL), lambda t: (norm_at(t) // n_blocks, norm_at(t) % n_blocks, 0)),
            pl.BlockSpec((MIX_SEQS, BLOCK, D_MODEL), lambda t: (mix_at(t) // n_blocks, mix_at(t) % n_blocks, 0)),
            pl.BlockSpec((MIX_SEQS, N_MOD, D_MODEL), lambda t: (norm_at(t) // n_blocks, 0, 0)),
            pl.BlockSpec((MIX_SEQS, N_MOD, D_MODEL), lambda t: (mix_at(t) // n_blocks, 0, 0)),
            pl.BlockSpec((1, D_MODEL), const),
            pl.BlockSpec((D_MODEL, IN_COLS), const),
            pl.BlockSpec((1, IN_COLS), const),
            pl.BlockSpec((1, ATTN_WIDTH), const),
            pl.BlockSpec((2, HGRN_WIDTH), const),
            pl.BlockSpec((1, HGRN_WIDTH), const),
            pl.BlockSpec((D_MODEL, D_MODEL), const),
            pl.BlockSpec((BLOCK, BLOCK), const),
        ],
        out_specs=pl.BlockSpec((MIX_SEQS, BLOCK, D_MODEL),
                               lambda t: (mix_at(t) // n_blocks, mix_at(t) % n_blocks, 0)),
        out_shape=jax.ShapeDtypeStruct(x.shape, F32),
        scratch_shapes=[
            pltpu.VMEM((MIX_SEQS * KV_VARIANTS, 2 * BLOCK, LANES_V7X), BF16),
            pltpu.VMEM((MIX_SEQS * HGRN_HEADS, HGRN_DIM, HGRN_DIM), F32),
            pltpu.VMEM((2, MIX_SEQS * BLOCK, D_MODEL), BF16),
            pltpu.VMEM((2, MIX_SEQS * BLOCK, IN_COLS), F32),
        ],
        compiler_params=pltpu.CompilerParams(
            dimension_semantics=("arbitrary",), vmem_limit_bytes=VMEM_LIMIT_BYTES),
        name="token_mixer",
    )(sinks, x, x, mod, mod, g_mix, w_in, b_in, attn_g, lb_logits, hgrn_g, w_out, tri)


FFN_ROWS = 1024
FFN_CHUNK = 256
CONV_HALO = 8


def _ffn_kernel(x_ref, mod_ref, g_ffn_ref, w_up_ref, conv_w_ref, conv_b_ref, w_down_ref, g_fin_ref,
                o_ref, carry_ref, ubuf_ref, act_ref):
    j = pl.program_id(1)

    @pl.when(j == 0)
    def _():
        carry_ref[...] = jnp.zeros_like(carry_ref)

    x = x_ref[0]
    shift = mod_ref[0, 3:4, :]
    scale = mod_ref[0, 4:5, :]
    gate = mod_ref[0, 5:6, :]
    h = (_rms_rows(x) * g_ffn_ref[...] * (1.0 + scale) + shift).astype(BF16)

    for c in range(D_FF // FFN_CHUNK):
        cols = slice(c * FFN_CHUNK, (c + 1) * FFN_CHUNK)
        u = _dot(h, w_up_ref[:, cols])
        v = _dot(h, w_up_ref[:, D_FF + c * FFN_CHUNK:D_FF + (c + 1) * FFN_CHUNK])
        ubuf_ref[0:CONV_HALO, :] = carry_ref[:, cols]
        ubuf_ref[CONV_HALO:, :] = u
        carry_ref[:, cols] = u[FFN_ROWS - CONV_HALO:, :]
        u1 = ubuf_ref[CONV_HALO - 1:CONV_HALO - 1 + FFN_ROWS, :]
        u2 = ubuf_ref[CONV_HALO - 2:CONV_HALO - 2 + FFN_ROWS, :]
        conv = (conv_w_ref[0:1, cols] * u2 + conv_w_ref[1:2, cols] * u1 + conv_w_ref[2:3, cols] * u
                + conv_b_ref[:, cols])
        act_ref[:, cols] = (_silu(conv) * v).astype(BF16)

    y = x + gate * _dot(act_ref[...], w_down_ref[...])
    o_ref[0] = _rms_rows(y) * g_fin_ref[...]


def _ffn_call(x1, mod, g_ffn, w_up, conv_w, conv_b, w_down, g_fin):
    bsz, seq, _ = x1.shape
    const = lambda b, j: (0, 0)
    return pl.pallas_call(
        _ffn_kernel,
        grid=(bsz, seq // FFN_ROWS),
        in_specs=[
            pl.BlockSpec((1, FFN_ROWS, D_MODEL), lambda b, j: (b, j, 0)),
            pl.BlockSpec((1, N_MOD, D_MODEL), lambda b, j: (b, 0, 0)),
            pl.BlockSpec((1, D_MODEL), const),
            pl.BlockSpec((D_MODEL, 2 * D_FF), const, pipeline_mode=pl.Buffered(1)),
            pl.BlockSpec((3, D_FF), const),
            pl.BlockSpec((1, D_FF), const),
            pl.BlockSpec((D_FF, D_MODEL), const, pipeline_mode=pl.Buffered(1)),
            pl.BlockSpec((1, D_MODEL), const),
        ],
        out_specs=pl.BlockSpec((1, FFN_ROWS, D_MODEL), lambda b, j: (b, j, 0)),
        out_shape=jax.ShapeDtypeStruct(x1.shape, F32),
        scratch_shapes=[
            pltpu.VMEM((CONV_HALO, D_FF), F32),
            pltpu.VMEM((CONV_HALO + FFN_ROWS, FFN_CHUNK), F32),
            pltpu.VMEM((FFN_ROWS, D_FF), BF16),
        ],
        compiler_params=pltpu.CompilerParams(
            dimension_semantics=("arbitrary", "arbitrary"), vmem_limit_bytes=VMEM_LIMIT_BYTES),
        name="channel_mixer",
    )(x1, mod, g_ffn, w_up, conv_w, conv_b, w_down, g_fin)


def kernel(x, c, ada_w, ada_b, mix_norm_g, w_in, b_in, attn_sinks, attn_out_g, hgrn_lb_logits, hgrn_out_g,
           w_out, ffn_norm_g, w_up, conv_w, conv_b, w_down, final_norm_g):
    assert ada_w.shape[0] == 1 and x.shape[-1] == D_MODEL and x.shape[1] % FFN_ROWS == 0
    assert x.shape[0] % MIX_SEQS == 0
    bsz = x.shape[0]
    mod = _ada_call(c, ada_w[0], ada_b[0][None, :]).reshape(bsz, N_MOD, D_MODEL)
    tri = jnp.asarray(_cumsum_matrix(), dtype=BF16)
    x1 = _mixer_call(
        x, mod, attn_sinks[0], mix_norm_g[0][None, :], w_in[0].astype(BF16), b_in[0][None, :],
        attn_out_g[0][None, :], hgrn_lb_logits, hgrn_out_g[0][None, :], w_out[0].astype(BF16), tri)
    return _ffn_call(
        x1, mod, ffn_norm_g[0][None, :], w_up[0].astype(BF16), conv_w[0], conv_b[0][None, :],
        w_down[0].astype(BF16), final_norm_g[None, :])
```

```python
import functools

import numpy as np
import jax
import jax.numpy as jnp
from jax import lax
from jax.experimental import pallas as pl
from jax.experimental.pallas import tpu as pltpu

D_MODEL = 1024
ATTN_HEAD_DIM = 64
ATTN_WIDTH = 512
ATTN_HEADS = 8
ATTN_KV_HEADS = 2
ATTN_KV_WIDTH = 128
WINDOW = 128
HGRN_HEADS = 4
HGRN_DIM = 128
HGRN_WIDTH = 512
IN_SPLITS = (ATTN_WIDTH, ATTN_KV_WIDTH, ATTN_KV_WIDTH, HGRN_WIDTH, HGRN_WIDTH, HGRN_WIDTH, HGRN_WIDTH)
IN_OFFSETS = tuple(int(s) for s in np.cumsum((0,) + IN_SPLITS))
IN_COLS = IN_OFFSETS[-1]
D_FF = 2816
N_MOD = 6
EPS = 1e-6
LOG2_E = 1.4426950408889634

BLOCK = 128
LOG2_BLOCK = 7
N_LEVELS = 6
SUBLANES_V7X = 8
LANES_V7X = 128
VMEM_LIMIT_BYTES = 56 * 1024 * 1024

F32 = jnp.float32
BF16 = jnp.bfloat16


def _dot(a, b):
    return jnp.dot(a, b, preferred_element_type=F32)


def _dot_nt(a, b):
    return lax.dot_general(a, b, (((1,), (1,)), ((), ())), preferred_element_type=F32)


def _split_bf16(v):
    hi = v.astype(BF16)
    lo = (v - hi.astype(F32)).astype(BF16)
    return hi, lo


def _rms_rows(v):
    return v * lax.rsqrt(jnp.mean(v * v, axis=-1, keepdims=True) + EPS)


def _silu(v):
    return v * jax.nn.sigmoid(v)


def _cumsum_matrix():
    t = np.arange(BLOCK)[:, None]
    r = np.arange(BLOCK)[None, :]
    return (r <= t).astype(np.float32)


def _rows_by_half(upper_src, lower_src, half):
    parts = [(upper_src if (start & half) else lower_src)[start:start + half] for start in range(0, BLOCK, half)]
    return jnp.concatenate(parts, axis=0)


def _level_decay(b, g, half, row):
    if half == 2:
        up1 = pltpu.roll(g, BLOCK - 1, 0)
        dn1 = pltpu.roll(g, 1, 0)
        pos = row & 3
        d = jnp.where(pos == 0, up1, jnp.where(pos == 1, 0.0, jnp.where(pos == 2, g, g + dn1)))
        return jnp.exp(d)
    if half == SUBLANES_V7X // 2:
        w = b.shape[-1]
        b3 = b.reshape(BLOCK // SUBLANES_V7X, SUBLANES_V7X, w)
        beta = jnp.broadcast_to(b3[:, half - 1:half, :], b3.shape).reshape(BLOCK, w)
        return jnp.exp(-jnp.abs(b - beta))
    parts = []
    for start in range(0, BLOCK, half):
        if start & half:
            ref = start - 1
            parts.append(b[start:start + half] - b[ref:ref + 1])
        else:
            ref = start + half - 1
            parts.append(b[ref:ref + 1] - b[start:start + half])
    return jnp.exp(jnp.concatenate(parts, axis=0))


def _pair_scores(a0, b0, a1, b1):
    z = jnp.zeros_like(b0)
    lhs = jnp.concatenate([a0, a1], axis=1)
    rhs = jnp.concatenate([jnp.concatenate([b0, z], axis=1), jnp.concatenate([z, b1], axis=1)], axis=0)
    out = _dot_nt(lhs, rhs)
    return out[:, :BLOCK], out[:, BLOCK:]


def _ada_kernel(c_ref, w_ref, b_ref, o_ref):
    cond = _silu(c_ref[...])
    c_hi, c_lo = _split_bf16(cond)
    w_hi, w_lo = _split_bf16(w_ref[...])
    o_ref[...] = _dot(c_hi, w_hi) + _dot(c_lo, w_hi) + _dot(c_hi, w_lo) + b_ref[...]


def _ada_call(c, w, b):
    bsz = c.shape[0]
    n = w.shape[1]
    tn = 512
    return pl.pallas_call(
        _ada_kernel,
        grid=(n // tn,),
        in_specs=[
            pl.BlockSpec((bsz, D_MODEL), lambda j: (0, 0)),
            pl.BlockSpec((D_MODEL, tn), lambda j: (0, j)),
            pl.BlockSpec((1, tn), lambda j: (0, j)),
        ],
        out_specs=pl.BlockSpec((bsz, tn), lambda j: (0, j)),
        out_shape=jax.ShapeDtypeStruct((bsz, n), F32),
        name="ada_mod",
    )(c, w, b)


PROJ_CHUNK = 256
MIX_SEQS = 4
KV_VARIANTS = 8


def _mixer_kernel(sink_ref, x_ref, xres_ref, mod_norm_ref, mod_mix_ref, g_mix_ref, w_in_ref, b_in_ref, attn_g_ref,
                  lb_logit_ref, hgrn_g_ref, w_out_ref, tri_ref, o_ref, kv_ref, state_ref, h_ref, proj_ref, *,
                  n_blocks):
    t = pl.program_id(0)
    mix_block = (t - 2) % n_blocks

    @pl.when((t == 0) | (mix_block == 0))
    def _():
        kv_ref[:, 0:BLOCK, :] = jnp.zeros((MIX_SEQS * KV_VARIANTS, BLOCK, LANES_V7X), BF16)
        state_ref[...] = jnp.zeros_like(state_ref)

    @pl.when(t == 0)
    def _():
        h_ref[1] = jnp.zeros((MIX_SEQS * BLOCK, D_MODEL), BF16)
        proj_ref[0] = jnp.zeros((MIX_SEQS * BLOCK, IN_COLS), F32)

    @pl.when(t % 2 == 0)
    def _():
        _mixer_step(mix_block == 0, 0, sink_ref, x_ref, xres_ref, mod_norm_ref, mod_mix_ref, g_mix_ref, w_in_ref,
                    b_in_ref, attn_g_ref, lb_logit_ref, hgrn_g_ref, w_out_ref, tri_ref, o_ref, kv_ref, state_ref,
                    h_ref, proj_ref)

    @pl.when(t % 2 == 1)
    def _():
        _mixer_step(mix_block == 0, 1, sink_ref, x_ref, xres_ref, mod_norm_ref, mod_mix_ref, g_mix_ref, w_in_ref,
                    b_in_ref, attn_g_ref, lb_logit_ref, hgrn_g_ref, w_out_ref, tri_ref, o_ref, kv_ref, state_ref,
                    h_ref, proj_ref)


def _mixer_step(first_block, parity, sink_ref, x_ref, xres_ref, mod_norm_ref, mod_mix_ref, g_mix_ref, w_in_ref,
                b_in_ref, attn_g_ref, lb_logit_ref, hgrn_g_ref, w_out_ref, tri_ref, o_ref, kv_ref, state_ref,
                h_ref, proj_ref):
    seqs = range(MIX_SEQS)
    mix_slot = parity
    proj_slot = 1 - parity
    norm_slot = parity

    shift = [mod_norm_ref[s, 0:1, :] for s in seqs]
    norm_gain = [g_mix_ref[...] * (1.0 + mod_norm_ref[s, 1:2, :]) for s in seqs]
    gate = [mod_mix_ref[s, 2:3, :] for s in seqs]

    l0 = lb_logit_ref[0:1, :]
    l1 = lb_logit_ref[1:2, :]
    lmax = jnp.maximum(l0, l1)
    e0 = jnp.exp(l0 - lmax)
    lb = e0 / (e0 + jnp.exp(l1 - lmax))
    tri = tri_ref[...]

    lane = lax.broadcasted_iota(jnp.int32, (BLOCK, LANES_V7X), 1)
    low = lane < ATTN_HEAD_DIM
    qi = lax.broadcasted_iota(jnp.int32, (BLOCK, 2 * BLOCK), 0)
    ki = lax.broadcasted_iota(jnp.int32, (BLOCK, 2 * BLOCK), 1)
    band = (ki > qi) & (ki <= qi + WINDOW)
    bias = jnp.where(band & (jnp.logical_not(first_block) | (ki >= BLOCK)), 0.0, -jnp.inf)
    bias2 = jnp.concatenate([bias, bias], axis=0)
    bias4 = jnp.concatenate([bias2, bias2], axis=1)

    ti = lax.broadcasted_iota(jnp.int32, (BLOCK, BLOCK), 0)
    si = lax.broadcasted_iota(jnp.int32, (BLOCK, BLOCK), 1)
    diff = ti ^ si
    lower = ti > si
    score_masks = [((diff >> (LOG2_BLOCK - 1 - l)) == 1) & lower for l in range(N_LEVELS)]
    diag_mask = ti == si
    adjacent_mask = (ti - si == 1) & ((ti & 1) == 1)
    roww = lax.broadcasted_iota(jnp.int32, (BLOCK, HGRN_WIDTH), 0)

    pending = list(range(IN_COLS // PROJ_CHUNK))

    def fill(count):
        for _ in range(min(count, len(pending))):
            c = pending.pop(0)
            cols = slice(c * PROJ_CHUNK, (c + 1) * PROJ_CHUNK)
            proj_ref[proj_slot, :, cols] = _dot(h_ref[proj_slot], w_in_ref[:, cols])

    def load(s, n):
        seg = slice(IN_OFFSETS[n], IN_OFFSETS[n + 1])
        return proj_ref[mix_slot, s * BLOCK:(s + 1) * BLOCK, seg] + b_in_ref[:, seg]

    fill(1)
    f, kf, g, b = [], [], [], []
    for s in seqs:
        f.append(lb + (1.0 - lb) * jax.nn.sigmoid(load(s, 4)))
        kf.append(1.0 - f[s])
        g.append(jnp.log(f[s]))
        g_hi, g_lo = _split_bf16(g[s])
        b.append(_dot(tri, g_hi) + _dot(tri, g_lo))

    logits = []
    for s in seqs:
        q_a, k_a, v_a = load(s, 0), load(s, 1), load(s, 2)
        zero = jnp.zeros((BLOCK, LANES_V7X), F32)
        k_roll = pltpu.roll(k_a, ATTN_HEAD_DIM, 1)
        v_roll = pltpu.roll(v_a, ATTN_HEAD_DIM, 1)
        variants = (
            (jnp.where(low, k_a, zero), jnp.where(low, v_a, zero)),
            (jnp.where(low, zero, k_roll), jnp.where(low, zero, v_roll)),
            (jnp.where(low, k_roll, zero), jnp.where(low, v_roll, zero)),
            (jnp.where(low, zero, k_a), jnp.where(low, zero, v_a)),
        )
        for n, (kk, vv) in enumerate(variants):
            kv_ref[KV_VARIANTS * s + n, BLOCK:2 * BLOCK, :] = kk.astype(BF16)
            kv_ref[KV_VARIANTS * s + 4 + n, BLOCK:2 * BLOCK, :] = vv.astype(BF16)
        q_scaled = (q_a * (ATTN_HEAD_DIM ** -0.5 * LOG2_E)).astype(BF16)
        per_kv = []
        for kvh in range(ATTN_KV_HEADS):
            qq = jnp.concatenate([q_scaled[:, (2 * kvh + r) * LANES_V7X:(2 * kvh + r + 1) * LANES_V7X]
                                  for r in range(2)], axis=0)
            first = KV_VARIANTS * s + 2 * kvh
            keys = kv_ref[first:first + 2].reshape(4 * BLOCK, LANES_V7X)
            per_kv.append(_dot_nt(qq, keys) + bias4)
        logits.append(per_kv)
        fill(1)
    q_inter, k_state, state_decay, lhs, rhs, gated, i_h, near = [], [], [], [], [], [], [], []
    for s in seqs:
        q_h = load(s, 3)
        i_h.append(load(s, 5))
        gated.append(_silu(load(s, 6)))
        b_last = b[s][BLOCK - 1:BLOCK, :]
        qf = _silu(q_h)
        q_inter.append((qf * jnp.exp(b[s])).astype(BF16))
        k_state.append((kf[s] * jnp.exp(b_last - b[s])).astype(BF16))
        state_decay.append(jnp.exp(b_last))
        operands = []
        for l in range(N_LEVELS):
            half = BLOCK >> (l + 1)
            if half >= SUBLANES_V7X:
                src = _rows_by_half(qf, kf[s], half)
            else:
                src = jnp.where((roww & half) != 0, qf, kf[s])
            operands.append((src * _level_decay(b[s], g[s], half, roww)).astype(BF16))
        lhs.append(operands)
        rhs.append(operands)
        near.append((qf * kf[s], qf * f[s] * pltpu.roll(kf[s], 1, 0)))
        fill(1)

    pairs = [[] for _ in seqs]
    scores = [[] for _ in seqs]
    for kvh in range(ATTN_KV_HEADS):
        for s in seqs:
            probs = [[None, None], [None, None]]
            inv = [[None, None], [None, None]]
            for half in range(2):
                for r in range(2):
                    sink = sink_ref[2 * (2 * kvh + r) + half] * LOG2_E
                    rows = logits[s][kvh][r * BLOCK:(r + 1) * BLOCK, half * 2 * BLOCK:(half + 1) * 2 * BLOCK]
                    m = jnp.maximum(jnp.max(rows, axis=-1, keepdims=True), sink)
                    e = jnp.exp2(rows - m)
                    inv[r][half] = 1.0 / (jnp.sum(e, axis=-1, keepdims=True) + jnp.exp2(sink - m))
                    probs[r][half] = e.astype(BF16)
            first = KV_VARIANTS * s + 4 + 2 * kvh
            o = _dot(jnp.concatenate([jnp.concatenate(probs[r], axis=1) for r in range(2)], axis=0),
                     kv_ref[first:first + 2].reshape(4 * BLOCK, LANES_V7X))
            for r in range(2):
                pairs[s].append(o[r * BLOCK:(r + 1) * BLOCK] * jnp.where(low, inv[r][0], inv[r][1]))
            for hh in (2 * kvh, 2 * kvh + 1):
                cols = slice(hh * HGRN_DIM, (hh + 1) * HGRN_DIM)
                sc = jnp.where(diag_mask, jnp.sum(near[s][0][:, cols], axis=-1, keepdims=True),
                               jnp.where(adjacent_mask, jnp.sum(near[s][1][:, cols], axis=-1, keepdims=True), 0.0))
                for n in range(0, N_LEVELS, 2):
                    s0, s1 = _pair_scores(lhs[s][n][:, cols], rhs[s][n][:, cols],
                                          lhs[s][n + 1][:, cols], rhs[s][n + 1][:, cols])
                    sc = jnp.where(score_masks[n], s0, jnp.where(score_masks[n + 1], s1, sc))
                scores[s].append(sc.astype(BF16))
                fill(1)
    attn = []
    for s in seqs:
        attn_s = jnp.concatenate(pairs[s], axis=-1)
        attn.append((_rms_rows(attn_s) * attn_g_ref[...]).astype(BF16))
    mixed = _dot(jnp.concatenate(attn, axis=0), w_out_ref[0:ATTN_WIDTH, :])
    for s in seqs:
        rows = slice(KV_VARIANTS * s, KV_VARIANTS * (s + 1))
        kv_ref[rows, 0:BLOCK, :] = kv_ref[rows, BLOCK:2 * BLOCK, :]

    recs = [[] for _ in seqs]
    for hh in range(HGRN_HEADS):
        cols = slice(hh * HGRN_DIM, (hh + 1) * HGRN_DIM)
        for s in seqs:
            v_t = i_h[s][:, cols].T.astype(BF16)
            state = state_ref[HGRN_HEADS * s + hh]
            o = _dot_nt(jnp.concatenate([q_inter[s][:, cols], scores[s][hh]], axis=1),
                        jnp.concatenate([state.astype(BF16), v_t], axis=1))
            state_ref[HGRN_HEADS * s + hh] = state * state_decay[s][:, cols] + _dot(v_t, k_state[s][:, cols])
            recs[s].append((_rms_rows(o) * hgrn_g_ref[:, cols] * gated[s][:, cols]).astype(BF16))

    rec = jnp.concatenate([jnp.concatenate(recs[s], axis=-1) for s in seqs], axis=0)
    mixed = mixed + _dot(rec, w_out_ref[ATTN_WIDTH:, :])
    fill(len(pending))
    for s in seqs:
        o_ref[s] = xres_ref[s] + gate[s] * mixed[s * BLOCK:(s + 1) * BLOCK]

    for s in seqs:
        h_ref[norm_slot, s * BLOCK:(s + 1) * BLOCK, :] = (_rms_rows(x_ref[s]) * norm_gain[s] + shift[s]).astype(BF16)


def _mixer_call(x, mod, sinks, g_mix, w_in, b_in, attn_g, lb_logits, hgrn_g, w_out, tri):
    bsz, seq, _ = x.shape
    n_blocks = seq // BLOCK
    n_steps = (bsz // MIX_SEQS) * n_blocks
    const = lambda t: (0, 0)
    norm_at = lambda t: jnp.minimum(t, n_steps - 1)
    mix_at = lambda t: jnp.maximum(t - 2, 0)
    return pl.pallas_call(
        functools.partial(_mixer_kernel, n_blocks=n_blocks),
        grid=(n_steps + 2,),
        in_specs=[
            pl.BlockSpec(memory_space=pltpu.SMEM),
            pl.BlockSpec((MIX_SEQS, BLOCK, D_MODEL), lambda t: (norm_at(t) // n_blocks, norm_at(t) % n_blocks, 0)),
            pl.BlockSpec((MIX_SEQS, BLOCK, D_MODEL), lambda t: (mix_at(t) // n_blocks, mix_at(t) % n_blocks, 0)),
            pl.BlockSpec((MIX_SEQS, N_MOD, D_MODEL), lambda t: (norm_at(t) // n_blocks, 0, 0)),
            pl.BlockSpec((MIX_SEQS, N_MOD, D_MODEL), lambda t: (mix_at(t) // n_blocks, 0, 0)),
            pl.BlockSpec((1, D_MODEL), const),
            pl.BlockSpec((D_MODEL, IN_COLS), const),
            pl.BlockSpec((1, IN_COLS), const),
            pl.BlockSpec((1, ATTN_WIDTH), const),
            pl.BlockSpec((2, HGRN_WIDTH), const),
            pl.BlockSpec((1, HGRN_WIDTH), const),
            pl.BlockSpec((D_MODEL, D_MODEL), const),
            pl.BlockSpec((BLOCK, BLOCK), const),
        ],
        out_specs=pl.BlockSpec((MIX_SEQS, BLOCK, D_MODEL),
                               lambda t: (mix_at(t) // n_blocks, mix_at(t) % n_blocks, 0)),
        out_shape=jax.ShapeDtypeStruct(x.shape, F32),
        scratch_shapes=[
            pltpu.VMEM((MIX_SEQS * KV_VARIANTS, 2 * BLOCK, LANES_V7X), BF16),
            pltpu.VMEM((MIX_SEQS * HGRN_HEADS, HGRN_DIM, HGRN_DIM), F32),
            pltpu.VMEM((2, MIX_SEQS * BLOCK, D_MODEL), BF16),
            pltpu.VMEM((2, MIX_SEQS * BLOCK, IN_COLS), F32),
        ],
        compiler_params=pltpu.CompilerParams(
            dimension_semantics=("arbitrary",), vmem_limit_bytes=VMEM_LIMIT_BYTES),
        name="token_mixer",
    )(sinks, x, x, mod, mod, g_mix, w_in, b_in, attn_g, lb_logits, hgrn_g, w_out, tri)


FFN_ROWS = 1024
FFN_CHUNK = 256
CONV_HALO = 8


def _ffn_kernel(x_ref, mod_ref, g_ffn_ref, w_up_ref, conv_w_ref, conv_b_ref, w_down_ref, g_fin_ref,
                o_ref, carry_ref, ubuf_ref, act_ref):
    j = pl.program_id(1)

    @pl.when(j == 0)
    def _():
        carry_ref[...] = jnp.zeros_like(carry_ref)

    x = x_ref[0]
    shift = mod_ref[0, 3:4, :]
    scale = mod_ref[0, 4:5, :]
    gate = mod_ref[0, 5:6, :]
    h = (_rms_rows(x) * g_ffn_ref[...] * (1.0 + scale) + shift).astype(BF16)

    for c in range(D_FF // FFN_CHUNK):
        cols = slice(c * FFN_CHUNK, (c + 1) * FFN_CHUNK)
        u = _dot(h, w_up_ref[:, cols])
        v = _dot(h, w_up_ref[:, D_FF + c * FFN_CHUNK:D_FF + (c + 1) * FFN_CHUNK])
        ubuf_ref[0:CONV_HALO, :] = carry_ref[:, cols]
        ubuf_ref[CONV_HALO:, :] = u
        carry_ref[:, cols] = u[FFN_ROWS - CONV_HALO:, :]
        u1 = ubuf_ref[CONV_HALO - 1:CONV_HALO - 1 + FFN_ROWS, :]
        u2 = ubuf_ref[CONV_HALO - 2:CONV_HALO - 2 + FFN_ROWS, :]
        conv = (conv_w_ref[0:1, cols] * u2 + conv_w_ref[1:2, cols] * u1 + conv_w_ref[2:3, cols] * u
                + conv_b_ref[:, cols])
        act_ref[:, cols] = (_silu(conv) * v).astype(BF16)

    y = x + gate * _dot(act_ref[...], w_down_ref[...])
    o_ref[0] = _rms_rows(y) * g_fin_ref[...]


def _ffn_call(x1, mod, g_ffn, w_up, conv_w, conv_b, w_down, g_fin):
    bsz, seq, _ = x1.shape
    const = lambda b, j: (0, 0)
    return pl.pallas_call(
        _ffn_kernel,
        grid=(bsz, seq // FFN_ROWS),
        in_specs=[
            pl.BlockSpec((1, FFN_ROWS, D_MODEL), lambda b, j: (b, j, 0)),
            pl.BlockSpec((1, N_MOD, D_MODEL), lambda b, j: (b, 0, 0)),
            pl.BlockSpec((1, D_MODEL), const),
            pl.BlockSpec((D_MODEL, 2 * D_FF), const, pipeline_mode=pl.Buffered(1)),
            pl.BlockSpec((3, D_FF), const),
            pl.BlockSpec((1, D_FF), const),
            pl.BlockSpec((D_FF, D_MODEL), const, pipeline_mode=pl.Buffered(1)),
            pl.BlockSpec((1, D_MODEL), const),
        ],
        out_specs=pl.BlockSpec((1, FFN_ROWS, D_MODEL), lambda b, j: (b, j, 0)),
        out_shape=jax.ShapeDtypeStruct(x1.shape, F32),
        scratch_shapes=[
            pltpu.VMEM((CONV_HALO, D_FF), F32),
            pltpu.VMEM((CONV_HALO + FFN_ROWS, FFN_CHUNK), F32),
            pltpu.VMEM((FFN_ROWS, D_FF), BF16),
        ],
        compiler_params=pltpu.CompilerParams(
            dimension_semantics=("arbitrary", "arbitrary"), vmem_limit_bytes=VMEM_LIMIT_BYTES),
        name="channel_mixer",
    )(x1, mod, g_ffn, w_up, conv_w, conv_b, w_down, g_fin)


def kernel(x, c, ada_w, ada_b, mix_norm_g, w_in, b_in, attn_sinks, attn_out_g, hgrn_lb_logits, hgrn_out_g,
           w_out, ffn_norm_g, w_up, conv_w, conv_b, w_down, final_norm_g):
    assert ada_w.shape[0] == 1 and x.shape[-1] == D_MODEL and x.shape[1] % FFN_ROWS == 0
    assert x.shape[0] % MIX_SEQS == 0
    bsz = x.shape[0]
    mod = _ada_call(c, ada_w[0], ada_b[0][None, :]).reshape(bsz, N_MOD, D_MODEL)
    tri = jnp.asarray(_cumsum_matrix(), dtype=BF16)
    x1 = _mixer_call(
        x, mod, attn_sinks[0], mix_norm_g[0][None, :], w_in[0].astype(BF16), b_in[0][None, :],
        attn_out_g[0][None, :], hgrn_lb_logits, hgrn_out_g[0][None, :], w_out[0].astype(BF16), tri)
    return _ffn_call(
        x1, mod, ffn_norm_g[0][None, :], w_up[0].astype(BF16), conv_w[0], conv_b[0][None, :],
        w_down[0].astype(BF16), final_norm_g[None, :])
```
